```python
import math
import jax
import jax.numpy as jnp
from jax import lax
import numpy as np

D_MODEL = 1024
BATCH = 4
SEQ = 4096
DEPTH = 1
DEC_BATCH = 128
DEC_SEQ = 4
PAST_LEN = 16384
PAGE_SIZE = 128

ATT_HEADS = 8
ATT_KV_HEADS = 2
ATT_HEAD_DIM = 64
ATT_GROUP = ATT_HEADS // ATT_KV_HEADS
WINDOW = 128
ROPE_THETA = 10000.0
DN_HEADS = 4
DN_HEAD_DIM = 128
CONV_WIDTH = 4
DN_CHUNK = 64
MEM_LEN = 256
X_HEADS = 4
X_HEAD_DIM = D_MODEL // X_HEADS
N_KEYS = 128
N_EXPERTS = N_KEYS * N_KEYS
PEER_HEADS = 8
PEER_QDIM = 256
PEER_TOPK = 16
PEER_BLOCK = 128
LN_EPS = 1e-5
RMS_EPS = 1e-6
NEG_INF = -1e30
DEEPNORM_ALPHA = (2 * DEPTH) ** 0.25
DEEPNORM_BETA = (8 * DEPTH) ** -0.25

ATT_Q = ATT_HEADS * ATT_HEAD_DIM
ATT_KV = ATT_KV_HEADS * ATT_HEAD_DIM
DN_W = DN_HEADS * DN_HEAD_DIM
MIX_WIDTH = ATT_Q + DN_W
CONV_CH = 3 * DN_W
IN_SIZES = (ATT_Q, ATT_KV, ATT_KV, CONV_CH, DN_HEADS, DN_HEADS, DN_W)
IN_SPLITS = tuple(int(s) for s in np.cumsum(IN_SIZES)[:-1])

kernel_name = 'hymba_swa_gdn_peer_step'


def layer_norm(x, g, b):
    xf = x.astype(jnp.float32)
    mu = jnp.mean(xf, -1, keepdims=True)
    var = jnp.mean(jnp.square(xf - mu), -1, keepdims=True)
    y = (xf - mu) * lax.rsqrt(var + LN_EPS) * g.astype(jnp.float32) + b.astype(jnp.float32)
    return y.astype(x.dtype)


def rope(x, pos):
    half = x.shape[-1] // 2
    inv_freq = ROPE_THETA ** (-jnp.arange(half, dtype=jnp.float32) / half)
    ang = pos.astype(jnp.float32)[:, None] * inv_freq[None, :]
    cos = jnp.cos(ang)[None, :, None, :]
    sin = jnp.sin(ang)[None, :, None, :]
    xf = x.astype(jnp.float32)
    x1, x2 = xf[..., :half], xf[..., half:]
    return jnp.concatenate([x1 * cos - x2 * sin, x2 * cos + x1 * sin], -1).astype(x.dtype)


def sink_softmax(s, sink):
    m = jnp.maximum(jnp.max(s, -1, keepdims=True), sink)
    p = jnp.exp(s - m)
    return p / (jnp.sum(p, -1, keepdims=True) + jnp.exp(sink - m))


def swa_banded(q, k, v, sinks):
    B, L = q.shape[:2]
    nb = L // WINDOW
    qb = q.reshape(B, nb, WINDOW, ATT_KV_HEADS, ATT_GROUP, ATT_HEAD_DIM)

    def band(t):
        tb = t.reshape(B, nb, WINDOW, ATT_KV_HEADS, ATT_HEAD_DIM)
        prev = jnp.concatenate([jnp.zeros_like(tb[:, :1]), tb[:, :-1]], 1)
        return jnp.concatenate([prev, tb], 2)

    kb, vb = band(k), band(v)
    s = jnp.einsum('bnqhgd,bnkhd->bnhgqk', qb, kb, preferred_element_type=jnp.float32) * ATT_HEAD_DIM ** -0.5
    qi = jnp.arange(WINDOW)[:, None]
    kj = jnp.arange(2 * WINDOW)[None, :]
    blk = jnp.arange(nb)[:, None, None]
    valid = (kj >= qi) & (kj <= qi + WINDOW) & (blk * WINDOW + kj >= WINDOW)
    s = jnp.where(valid[None, :, None, None], s, NEG_INF)
    p = sink_softmax(s, sinks.astype(jnp.float32).reshape(ATT_KV_HEADS, ATT_GROUP, 1, 1))
    o = jnp.einsum('bnhgqk,bnkhd->bnqhgd', p.astype(vb.dtype), vb)
    return o.reshape(B, L, ATT_Q)


def swa_cached(q, k, v, past_k, past_v, sinks):
    B, L = q.shape[:2]
    P = past_k.shape[1]
    kk = jnp.concatenate([past_k.astype(k.dtype), k], 1)
    vv = jnp.concatenate([past_v.astype(v.dtype), v], 1)
    qpos = jnp.arange(L)[:, None]
    kpos = jnp.arange(P + L)[None, :] - P
    valid = (kpos <= qpos) & (kpos >= qpos - WINDOW)
    qg = q.reshape(B, L, ATT_KV_HEADS, ATT_GROUP, ATT_HEAD_DIM)
    s = jnp.einsum('blhgd,bkhd->bhglk', qg, kk, preferred_element_type=jnp.float32) * ATT_HEAD_DIM ** -0.5
    s = jnp.where(valid, s, NEG_INF)
    p = sink_softmax(s, sinks.astype(jnp.float32).reshape(ATT_KV_HEADS, ATT_GROUP, 1, 1))
    o = jnp.einsum('bhglk,bkhd->blhgd', p.astype(vv.dtype), vv)
    return o.reshape(B, L, ATT_Q), kk[:, -P:], vv[:, -P:]


def causal_conv(xc, buf, w_conv):
    L = xc.shape[1]
    xp = jnp.concatenate([buf.astype(xc.dtype), xc], 1)
    y = xp[:, 0:L] * w_conv[0]
    for j in range(1, CONV_WIDTH):
        y = y + xp[:, j:j + L] * w_conv[j]
    return jax.nn.silu(y), xp[:, -(CONV_WIDTH - 1):]


def l2norm(t):
    return t * lax.rsqrt(jnp.sum(jnp.square(t), -1, keepdims=True) + RMS_EPS)


def gated_delta_chunked(q, k, v, g, beta, S0, C):
    B, L, H, _ = q.shape
    dv = v.shape[-1]
    N = L // C

    def blk(t):
        return jnp.moveaxis(t.reshape((B, N, C, H) + t.shape[3:]), 3, 2)

    q, k, v, g, beta = blk(q), blk(k), blk(v), blk(g), blk(beta)
    gc = jnp.cumsum(g, -1)
    tri_incl = jnp.tril(jnp.ones((C, C), dtype=bool))
    tri_strict = jnp.tril(jnp.ones((C, C), dtype=bool), -1)
    decay = jnp.exp(jnp.where(tri_incl, gc[..., :, None] - gc[..., None, :], -jnp.inf))
    k_beta = k * beta[..., None]
    A = jnp.where(tri_strict, jnp.einsum('bnhid,bnhjd->bnhij', k_beta, k) * decay, 0.0)
    rhs = jnp.concatenate([v * beta[..., None], k_beta * jnp.exp(gc)[..., None]], -1)
    sol = lax.linalg.triangular_solve(jnp.eye(C, dtype=jnp.float32) + A, rhs, left_side=True, lower=True)
    u, w = sol[..., :dv], sol[..., dv:]
    qk = jnp.einsum('bnhid,bnhjd->bnhij', q, k) * decay

    def step(S, xs):
        q_i, k_i, u_i, w_i, gc_i, qk_i = xs
        v_new = u_i - jnp.einsum('bhck,bhkv->bhcv', w_i, S)
        o = (jnp.einsum('bhck,bhkv->bhcv', q_i * jnp.exp(gc_i)[..., None], S)
             + jnp.einsum('bhij,bhjv->bhiv', qk_i, v_new))
        g_last = gc_i[..., -1]
        S = (S * jnp.exp(g_last)[..., None, None]
             + jnp.einsum('bhck,bhcv->bhkv', k_i * jnp.exp(g_last[..., None] - gc_i)[..., None], v_new))
        return S, o

    xs = tuple(jnp.moveaxis(t, 1, 0) for t in (q, k, u, w, gc, qk))
    S, o = lax.scan(step, S0, xs)
    o = jnp.swapaxes(jnp.moveaxis(o, 0, 1), 2, 3).reshape(B, L, H, dv)
    return o, S


def memory_kv(mem, w_ck, w_cv):
    B, M, _ = mem.shape
    mk = (mem @ w_ck).reshape(B, M, X_HEADS, X_HEAD_DIM)
    mv = (mem @ w_cv).reshape(B, M, X_HEADS, X_HEAD_DIM)
    return mk, mv


def cross_attention(x, mem_k, mem_v, w_cq, w_co):
    B, L, _ = x.shape
    q = (x @ w_cq).reshape(B, L, X_HEADS, X_HEAD_DIM)
    s = jnp.einsum('blhd,bmhd->bhlm', q, mem_k.astype(q.dtype), preferred_element_type=jnp.float32) * X_HEAD_DIM ** -0.5
    p = jax.nn.softmax(s, -1)
    o = jnp.einsum('bhlm,bmhd->blhd', p.astype(x.dtype), mem_v.astype(x.dtype))
    return o.reshape(B, L, D_MODEL) @ w_co


def peer(x, w_pq, keys1, keys2, expert_u, expert_v):
    T = x.shape[0]
    n_blk = -(-T // PEER_BLOCK)
    xb = jnp.pad(x, ((0, n_blk * PEER_BLOCK - T), (0, 0))).reshape(n_blk, PEER_BLOCK, D_MODEL)
    half = PEER_QDIM // 2

    def block(xt):
        q = jnp.einsum('td,dhq->thq', xt, w_pq, preferred_element_type=jnp.float32)
        s1 = jnp.einsum('thc,kc->thk', q[..., :half], keys1.astype(jnp.float32))
        s2 = jnp.einsum('thc,kc->thk', q[..., half:], keys2.astype(jnp.float32))
        v1, i1 = lax.top_k(s1, PEER_TOPK)
        v2, i2 = lax.top_k(s2, PEER_TOPK)
        cand = (v1[..., :, None] + v2[..., None, :]).reshape(PEER_BLOCK, PEER_HEADS, PEER_TOPK * PEER_TOPK)
        sc, ci = lax.top_k(cand, PEER_TOPK)
        eid = (jnp.take_along_axis(i1, ci // PEER_TOPK, -1) * N_KEYS
               + jnp.take_along_axis(i2, ci % PEER_TOPK, -1))
        gate = jax.nn.softmax(sc, -1)
        hid = jax.nn.gelu(jnp.einsum('td,thkd->thk', xt, expert_u[eid], preferred_element_type=jnp.float32))
        return jnp.einsum('thk,thkd->td', (gate * hid).astype(xt.dtype), expert_v[eid])

    return lax.map(block, xb).reshape(n_blk * PEER_BLOCK, D_MODEL)[:T]


def decoder_layer(x, pos0, swa_past_k, swa_past_v, dn_s0, conv_buf, mem_k, mem_v,
                  w_in, w_conv, dn_a_log, dn_dt_bias, dn_norm_g, attn_sinks, w_o, ln1_g, ln1_b,
                  w_cq, w_co, ln2_g, ln2_b, w_pq, peer_keys1, peer_keys2, peer_u, peer_v, ln3_g, ln3_b):
    B, L, _ = x.shape
    f32 = jnp.float32
    pos = pos0 + jnp.arange(L, dtype=jnp.int32)
    h = x @ w_in
    aq, ak, av, dqkv, db, da, dz = jnp.split(h, IN_SPLITS, axis=-1)
    aq = rope(aq.reshape(B, L, ATT_HEADS, ATT_HEAD_DIM), pos)
    ak = rope(ak.reshape(B, L, ATT_KV_HEADS, ATT_HEAD_DIM), pos)
    av = av.reshape(B, L, ATT_KV_HEADS, ATT_HEAD_DIM)
    if swa_past_k is None:
        a_out = swa_banded(aq, ak, av, attn_sinks)
        new_k, new_v = ak[:, -WINDOW:], av[:, -WINDOW:]
    else:
        a_out, new_k, new_v = swa_cached(aq, ak, av, swa_past_k, swa_past_v, attn_sinks)
    c_out, new_conv = causal_conv(dqkv, conv_buf, w_conv)
    dq, dk, dv = jnp.split(c_out.astype(f32).reshape(B, L, 3 * DN_HEADS, DN_HEAD_DIM), 3, axis=2)
    dq = l2norm(dq) * DN_HEAD_DIM ** -0.5
    dk = l2norm(dk)
    beta = jax.nn.sigmoid(db.astype(f32))
    g = -jnp.exp(dn_a_log.astype(f32)) * jax.nn.softplus(da.astype(f32) + dn_dt_bias.astype(f32))
    chunk = DN_CHUNK if L % DN_CHUNK == 0 else L
    o, S = gated_delta_chunked(dq, dk, dv, g, beta, dn_s0.astype(f32), chunk)
    o = o * lax.rsqrt(jnp.mean(jnp.square(o), -1, keepdims=True) + RMS_EPS) * dn_norm_g.astype(f32)
    o = o * jax.nn.silu(dz.astype(f32).reshape(B, L, DN_HEADS, DN_HEAD_DIM))
    mix = jnp.concatenate([a_out, o.reshape(B, L, DN_W).astype(x.dtype)], -1) @ w_o
    x1 = layer_norm(DEEPNORM_ALPHA * x + mix, ln1_g, ln1_b)
    x2 = layer_norm(DEEPNORM_ALPHA * x1 + cross_attention(x1, mem_k, mem_v, w_cq, w_co), ln2_g, ln2_b)
    ffn = peer(x2.reshape(B * L, D_MODEL), w_pq, peer_keys1, peer_keys2, peer_u, peer_v).reshape(B, L, D_MODEL)
    x3 = layer_norm(DEEPNORM_ALPHA * x2 + ffn, ln3_g, ln3_b)
    return x3, new_k, new_v, S.astype(x.dtype), new_conv.astype(x.dtype)


def setup_inputs(seed: int = 0) -> dict:
    key = jax.random.key(seed)
    keys = iter(jax.random.split(key, 48))
    f32 = jnp.float32

    def normal(shape, scale):
        return jax.random.normal(next(keys), shape, f32) * scale

    s_in = D_MODEL ** -0.5
    beta = DEEPNORM_BETA
    x_prompt = normal((BATCH, SEQ, D_MODEL), 1.0)
    x_sample = normal((DEC_BATCH, DEC_SEQ, D_MODEL), 1.0)
    cache_swa_k = normal((DEPTH, DEC_BATCH, WINDOW, ATT_KV_HEADS, ATT_HEAD_DIM), 1.0)
    cache_swa_v = normal((DEPTH, DEC_BATCH, WINDOW, ATT_KV_HEADS, ATT_HEAD_DIM), beta)
    state_dn = normal((DEPTH, DEC_BATCH, DN_HEADS, DN_HEAD_DIM, DN_HEAD_DIM), 0.1)
    state_dn_conv = normal((DEPTH, DEC_BATCH, CONV_WIDTH - 1, CONV_CH), 1.0)
    cache_mem_k = normal((DEPTH, DEC_BATCH, MEM_LEN, X_HEADS, X_HEAD_DIM), 1.0)
    cache_mem_v = normal((DEPTH, DEC_BATCH, MEM_LEN, X_HEADS, X_HEAD_DIM), beta)
    mem_prompt = normal((BATCH, MEM_LEN, D_MODEL), 1.0)
    w_in = jnp.concatenate([
        normal((DEPTH, D_MODEL, ATT_Q), s_in),
        normal((DEPTH, D_MODEL, ATT_KV), s_in),
        normal((DEPTH, D_MODEL, ATT_KV), s_in * beta),
        normal((DEPTH, D_MODEL, 2 * DN_W), s_in),
        normal((DEPTH, D_MODEL, DN_W), s_in * beta),
        normal((DEPTH, D_MODEL, 2 * DN_HEADS), s_in),
        normal((DEPTH, D_MODEL, DN_W), s_in)], axis=-1)
    w_conv = normal((DEPTH, CONV_WIDTH, CONV_CH), CONV_WIDTH ** -0.5)
    dn_a_log = jnp.log(jax.random.uniform(next(keys), (DEPTH, DN_HEADS), f32, 1.0, 16.0))
    dt = jnp.exp(jax.random.uniform(next(keys), (DEPTH, DN_HEADS), f32, math.log(1e-3), math.log(1e-1)))
    dn_dt_bias = dt + jnp.log(-jnp.expm1(-dt))
    dn_norm_g = 1.0 + normal((DEPTH, DN_HEAD_DIM), 0.05)
    attn_sinks = normal((DEPTH, ATT_HEADS), 0.5)
    w_o = normal((DEPTH, MIX_WIDTH, D_MODEL), MIX_WIDTH ** -0.5 * beta)
    ln1_g = 1.0 + normal((DEPTH, D_MODEL), 0.05)
    ln1_b = normal((DEPTH, D_MODEL), 0.02)
    w_cq = normal((DEPTH, D_MODEL, D_MODEL), s_in)
    w_ck = normal((DEPTH, D_MODEL, D_MODEL), s_in)
    w_cv = normal((DEPTH, D_MODEL, D_MODEL), s_in * beta)
    w_co = normal((DEPTH, D_MODEL, D_MODEL), s_in * beta)
    ln2_g = 1.0 + normal((DEPTH, D_MODEL), 0.05)
    ln2_b = normal((DEPTH, D_MODEL), 0.02)
    w_pq = normal((DEPTH, D_MODEL, PEER_HEADS, PEER_QDIM), s_in)
    peer_keys1 = normal((DEPTH, N_KEYS, PEER_QDIM // 2), (PEER_QDIM // 2) ** -0.5)
    peer_keys2 = normal((DEPTH, N_KEYS, PEER_QDIM // 2), (PEER_QDIM // 2) ** -0.5)
    peer_u = normal((DEPTH, N_EXPERTS, D_MODEL), s_in)
    peer_v = normal((DEPTH, N_EXPERTS, D_MODEL), beta)
    ln3_g = 1.0 + normal((DEPTH, D_MODEL), 0.05)
    ln3_b = normal((DEPTH, D_MODEL), 0.02)
    return {
        'x_prompt': x_prompt, 'x_sample': x_sample,
        'cache_swa_k': cache_swa_k, 'cache_swa_v': cache_swa_v,
        'state_dn': state_dn, 'state_dn_conv': state_dn_conv,
        'cache_mem_k': cache_mem_k, 'cache_mem_v': cache_mem_v,
        'mem_prompt': mem_prompt,
        'w_in': w_in, 'w_conv': w_conv, 'dn_a_log': dn_a_log, 'dn_dt_bias': dn_dt_bias,
        'dn_norm_g': dn_norm_g, 'attn_sinks': attn_sinks, 'w_o': w_o, 'ln1_g': ln1_g, 'ln1_b': ln1_b,
        'w_cq': w_cq, 'w_ck': w_ck, 'w_cv': w_cv, 'w_co': w_co, 'ln2_g': ln2_g, 'ln2_b': ln2_b,
        'w_pq': w_pq, 'peer_keys1': peer_keys1, 'peer_keys2': peer_keys2,
        'peer_u': peer_u, 'peer_v': peer_v, 'ln3_g': ln3_g, 'ln3_b': ln3_b,
    }


def reference(x_prompt, x_sample, cache_swa_k, cache_swa_v, state_dn, state_dn_conv, cache_mem_k, cache_mem_v,
              mem_prompt, w_in, w_conv, dn_a_log, dn_dt_bias, dn_norm_g, attn_sinks, w_o, ln1_g, ln1_b,
              w_cq, w_ck, w_cv, w_co, ln2_g, ln2_b, w_pq, peer_keys1, peer_keys2, peer_u, peer_v, ln3_g, ln3_b):
    B = x_prompt.shape[0]
    zero_state = jnp.zeros((B, DN_HEADS, DN_HEAD_DIM, DN_HEAD_DIM), jnp.float32)
    zero_conv = jnp.zeros((B, CONV_WIDTH - 1, CONV_CH), x_prompt.dtype)
    y_prompt, y_sample = x_prompt, x_sample
    p_k, p_v, p_s, p_c, p_mk, p_mv = [], [], [], [], [], []
    s_k, s_v, s_s, s_c = [], [], [], []
    for l in range(DEPTH):
        lw = (w_in[l], w_conv[l], dn_a_log[l], dn_dt_bias[l], dn_norm_g[l], attn_sinks[l], w_o[l],
              ln1_g[l], ln1_b[l], w_cq[l], w_co[l], ln2_g[l], ln2_b[l], w_pq[l], peer_keys1[l],
              peer_keys2[l], peer_u[l], peer_v[l], ln3_g[l], ln3_b[l])
        mk, mv = memory_kv(mem_prompt, w_ck[l], w_cv[l])
        y_prompt, pk, pv, ps, pc = decoder_layer(y_prompt, 0, None, None, zero_state, zero_conv, mk, mv, *lw)
        y_sample, sk, sv, ss, sc = decoder_layer(y_sample, PAST_LEN, cache_swa_k[l], cache_swa_v[l], state_dn[l],
                                                 state_dn_conv[l], cache_mem_k[l], cache_mem_v[l], *lw)
        p_k.append(pk); p_v.append(pv); p_s.append(ps); p_c.append(pc); p_mk.append(mk); p_mv.append(mv)
        s_k.append(sk); s_v.append(sv); s_s.append(ss); s_c.append(sc)
    swa_k_prompt, swa_v_prompt = jnp.stack(p_k), jnp.stack(p_v)
    dn_state_prompt, dn_conv_prompt = jnp.stack(p_s), jnp.stack(p_c)
    mem_k_prompt, mem_v_prompt = jnp.stack(p_mk), jnp.stack(p_mv)
    swa_k_sample, swa_v_sample = jnp.stack(s_k), jnp.stack(s_v)
    dn_state_sample, dn_conv_sample = jnp.stack(s_s), jnp.stack(s_c)
    return (y_prompt, y_sample, swa_k_prompt, swa_v_prompt, dn_state_prompt, dn_conv_prompt,
            mem_k_prompt, mem_v_prompt, swa_k_sample, swa_v_sample, dn_state_sample, dn_conv_sample)
```

```python
import functools
import math

import jax
import jax.numpy as jnp
from jax import lax
from jax.experimental import pallas as pl
from jax.experimental.pallas import tpu as pltpu

F32 = jnp.float32
BF16 = jnp.bfloat16

D_MODEL = 1024
DEPTH = 1
PAST_LEN = 16384
ATT_HEADS = 8
ATT_KV_HEADS = 2
ATT_HEAD_DIM = 64
ATT_GROUP = ATT_HEADS // ATT_KV_HEADS
WINDOW = 128
ROPE_THETA = 10000.0
DN_HEADS = 4
DN_HEAD_DIM = 128
CONV_WIDTH = 4
DN_CHUNK = 64
MEM_LEN = 256
X_HEADS = 4
X_HEAD_DIM = D_MODEL // X_HEADS
N_KEYS = 128
PEER_HEADS = 8
PEER_QDIM = 256
PEER_TOPK = 16
LN_EPS = 1e-5
RMS_EPS = 1e-6
NEG_INF = -1e30
DEEPNORM_ALPHA = (2 * DEPTH) ** 0.25

ATT_Q = ATT_HEADS * ATT_HEAD_DIM
ATT_KV = ATT_KV_HEADS * ATT_HEAD_DIM
DN_W = DN_HEADS * DN_HEAD_DIM
CONV_CH = 3 * DN_W
IN_PACKED = ATT_Q + 2 * ATT_KV + CONV_CH + DN_W + 128

LANES = 128
SUBLANES = 8
VMEM_LIMIT = 48 * 1024 * 1024
SAMPLE_PAD = 8


def _cparams(sem):
    return pltpu.CompilerParams(dimension_semantics=sem, vmem_limit_bytes=VMEM_LIMIT)


def _dot(a, b):
    return jnp.dot(a, b, preferred_element_type=F32)


def _dot_nt(a, b):
    return lax.dot_general(a, b, (((1,), (1,)), ((), ())), preferred_element_type=F32)


def _dot_tn(a, b):
    return lax.dot_general(a, b, (((0,), (0,)), ((), ())), preferred_element_type=F32)


def _split3(x):
    x0 = x.astype(BF16)
    r = x - x0.astype(F32)
    x1 = r.astype(BF16)
    x2 = (r - x1.astype(F32)).astype(BF16)
    return x0, x1, x2


def _dot_exact_lhs(a_bf16, b):
    b0, b1, b2 = _split3(b)
    return _dot(a_bf16, b0) + _dot(a_bf16, b1) + _dot(a_bf16, b2)


def _dot_exact_rhs(a, b_bf16):
    a0, a1, a2 = _split3(a)
    return _dot(a0, b_bf16) + _dot(a1, b_bf16) + _dot(a2, b_bf16)


def _dot_hi(a, b):
    a0 = a.astype(BF16)
    a1 = (a - a0.astype(F32)).astype(BF16)
    b0 = b.astype(BF16)
    b1 = (b - b0.astype(F32)).astype(BF16)
    return _dot(a0, b0) + _dot(a0, b1) + _dot(a1, b0)


def _sigmoid(x):
    return 1.0 / (1.0 + jnp.exp(-x))


def _layer_norm(y, g, b):
    mu = jnp.mean(y, -1, keepdims=True)
    d = y - mu
    var = jnp.mean(d * d, -1, keepdims=True)
    return d * lax.rsqrt(var + LN_EPS) * g + b


def _in_proj_kernel(x_ref, w_ref, wt_ref, cos_ref, sin_ref,
                    q_ref, k_ref, v_ref, dqkv_ref, dz_ref, ba_ref, bat_ref):
    xb = x_ref[...].astype(BF16)
    h = _dot(xb, w_ref[...])
    cos = cos_ref[...]
    sin = sin_ref[...]
    lane = lax.broadcasted_iota(jnp.int32, cos.shape, 1)
    first = (lane & (ATT_HEAD_DIM - 1)) < (ATT_HEAD_DIM // 2)

    def rope(t):
        up = pltpu.roll(t, LANES - ATT_HEAD_DIM // 2, 1)
        dn = pltpu.roll(t, ATT_HEAD_DIM // 2, 1)
        return t * cos + jnp.where(first, up, dn) * sin

    for j in range(ATT_Q // LANES):
        q_ref[:, j * LANES:(j + 1) * LANES] = rope(h[:, j * LANES:(j + 1) * LANES])
    k_ref[...] = rope(h[:, ATT_Q:ATT_Q + ATT_KV])
    o = ATT_Q + ATT_KV
    v_ref[...] = h[:, o:o + ATT_KV]
    o += ATT_KV
    dqkv_ref[...] = h[:, o:o + CONV_CH]
    o += CONV_CH
    dz_ref[...] = h[:, o:o + DN_W]
    o += DN_W
    ba_ref[...] = h[:, o:o + LANES]
    bat_ref[...] = _dot_nt(wt_ref[...], xb)


def _in_proj(x, w_p, w_t, cos, sin, tm):
    t = x.shape[0]
    nt = t // tm
    ncos = cos.shape[0] // tm
    row = lambda width: pl.BlockSpec((tm, width), lambda i: (i, 0))
    out_shape = (
        jax.ShapeDtypeStruct((t, ATT_Q), F32),
        jax.ShapeDtypeStruct((t, ATT_KV), F32),
        jax.ShapeDtypeStruct((t, ATT_KV), F32),
        jax.ShapeDtypeStruct((t, CONV_CH), F32),
        jax.ShapeDtypeStruct((t, DN_W), F32),
        jax.ShapeDtypeStruct((t, LANES), F32),
        jax.ShapeDtypeStruct((SUBLANES, t), F32),
    )
    return pl.pallas_call(
        _in_proj_kernel,
        grid=(nt,),
        in_specs=[
            row(D_MODEL),
            pl.BlockSpec((D_MODEL, IN_PACKED), lambda i: (0, 0)),
            pl.BlockSpec((SUBLANES, D_MODEL), lambda i: (0, 0)),
            pl.BlockSpec((tm, LANES), lambda i: (i % ncos, 0)),
            pl.BlockSpec((tm, LANES), lambda i: (i % ncos, 0)),
        ],
        out_specs=(row(ATT_Q), row(ATT_KV), row(ATT_KV), row(CONV_CH), row(DN_W), row(LANES),
                   pl.BlockSpec((SUBLANES, tm), lambda i: (0, i))),
        out_shape=out_shape,
        compiler_params=_cparams(("parallel",)),
        name="in_proj",
    )(x, w_p, w_t, cos, sin)


def _swa_kernel(sinks_ref, q_ref, *refs, has_prev):
    if has_prev:
        kp_ref, kc_ref, vp_ref, vc_ref, o_ref = refs
        k = jnp.concatenate([kp_ref[...], kc_ref[...]], 0)
        v = jnp.concatenate([vp_ref[...], vc_ref[...]], 0)
    else:
        kc_ref, vc_ref, o_ref = refs
        k = kc_ref[...]
        v = vc_ref[...]
    q = q_ref[...]
    r, nk = q.shape[0], k.shape[0]
    qi = lax.broadcasted_iota(jnp.int32, (r, nk), 0)
    kj = lax.broadcasted_iota(jnp.int32, (r, nk), 1)
    valid = (kj >= qi) & (kj <= qi + WINDOW)
    if has_prev:
        jmin = jnp.where(pl.program_id(1) == 0, WINDOW, 0)
        valid = valid & (kj >= jmin)
    kb = k.astype(BF16)
    vb = v.astype(BF16)
    outs = []
    for h in range(ATT_HEADS):
        kvh = h // ATT_GROUP
        qh = q[:, h * ATT_HEAD_DIM:(h + 1) * ATT_HEAD_DIM].astype(BF16)
        kh = kb[:, kvh * ATT_HEAD_DIM:(kvh + 1) * ATT_HEAD_DIM]
        s = _dot_nt(qh, kh) * (ATT_HEAD_DIM ** -0.5)
        s = jnp.where(valid, s, NEG_INF)
        sink = sinks_ref[h]
        m = jnp.maximum(jnp.max(s, -1, keepdims=True), sink)
        p = jnp.exp(s - m)
        den = jnp.sum(p, -1, keepdims=True) + jnp.exp(sink - m)
        p = p * (1.0 / den)
        outs.append(_dot(p.astype(BF16), vb[:, kvh * ATT_HEAD_DIM:(kvh + 1) * ATT_HEAD_DIM]))
    o_ref[...] = jnp.concatenate(outs, -1)


def _swa(sinks, q, k, v, nb, nblk, rq, nk, has_prev):
    smem = pl.BlockSpec(memory_space=pltpu.SMEM)
    qspec = pl.BlockSpec((rq, ATT_Q), lambda b, n: (b * nblk + n, 0))
    cur = pl.BlockSpec((nk, ATT_KV), lambda b, n: (b * nblk + n, 0))
    if has_prev:
        prev = pl.BlockSpec((nk, ATT_KV), lambda b, n: (b * nblk + jnp.maximum(n - 1, 0), 0))
        in_specs = [smem, qspec, prev, cur, prev, cur]
        args = (sinks, q, k, k, v, v)
    else:
        in_specs = [smem, qspec, cur, cur]
        args = (sinks, q, k, v)
    return pl.pallas_call(
        functools.partial(_swa_kernel, has_prev=has_prev),
        grid=(nb, nblk),
        in_specs=in_specs,
        out_specs=qspec,
        out_shape=jax.ShapeDtypeStruct(q.shape, F32),
        compiler_params=_cparams(("parallel", "arbitrary")),
        name="swa_prev" if has_prev else "swa_cache",
    )(*args)


def _deltanet_kernel(x_ref, tail_ref, wc_ref, ba_ref, bat_ref, dz_ref, s0_ref,
                     pc_ref, pr_ref, ng_ref, o_ref, s_ref, s_scr, xp_scr, *, c, c_real, nsq):
    n = pl.program_id(1)

    @pl.when(n == 0)
    def _():
        s_scr[...] = s0_ref[0]
        xp_scr[0:SUBLANES] = tail_ref[...]

    xp_scr[SUBLANES:SUBLANES + c] = x_ref[...]
    wc = wc_ref[...]
    base = SUBLANES - (CONV_WIDTH - 1)
    y = xp_scr[base:base + c] * wc[0:1]
    for j in range(1, CONV_WIDTH):
        y = y + xp_scr[base + j:base + j + c] * wc[j:j + 1]
    y = y * _sigmoid(y)
    xp_scr[0:SUBLANES] = xp_scr[c:c + SUBLANES]

    ba = ba_ref[...]
    bat = bat_ref[0]
    beta_c = _sigmoid(ba)
    g_c = -jnp.exp(pc_ref[0:1]) * _softplus(ba + pc_ref[1:2])
    beta_r = _sigmoid(bat)
    g_r = -jnp.exp(pr_ref[0][:, 0:c]) * _softplus(bat + pr_ref[1][:, 0:c])
    if c_real < c:
        rmask = lax.broadcasted_iota(jnp.int32, ba.shape, 0) < c_real
        beta_c = jnp.where(rmask, beta_c, 0.0)
        g_c = jnp.where(rmask, g_c, 0.0)
        lmask = lax.broadcasted_iota(jnp.int32, bat.shape, 1) < c_real
        beta_r = jnp.where(lmask, beta_r, 0.0)
        g_r = jnp.where(lmask, g_r, 0.0)

    ii = lax.broadcasted_iota(jnp.int32, (c, c), 0)
    jj = lax.broadcasted_iota(jnp.int32, (c, c), 1)
    tri = jnp.where(ii >= jj, 1.0, 0.0).astype(BF16)
    tri_t = jnp.where(ii <= jj, 1.0, 0.0).astype(BF16)
    gc_c = _dot_exact_lhs(tri, g_c)
    gc_r = _dot_exact_rhs(g_r, tri_t)
    eye = jnp.where(ii == jj, 1.0, 0.0)
    incl = ii >= jj
    strict = ii > jj

    ng = ng_ref[...]
    dz = dz_ref[...]
    outs = []
    for h in range(DN_HEADS):
        lo = h * DN_HEAD_DIM
        q = y[:, lo:lo + DN_HEAD_DIM]
        k = y[:, DN_W + lo:DN_W + lo + DN_HEAD_DIM]
        v = y[:, 2 * DN_W + lo:2 * DN_W + lo + DN_HEAD_DIM]
        q = q * lax.rsqrt(jnp.sum(q * q, -1, keepdims=True) + RMS_EPS) * (DN_HEAD_DIM ** -0.5)
        k = k * lax.rsqrt(jnp.sum(k * k, -1, keepdims=True) + RMS_EPS)
        beta = beta_c[:, h:h + 1]
        gcc = gc_c[:, DN_HEADS + h:DN_HEADS + h + 1]
        gcr = gc_r[DN_HEADS + h:DN_HEADS + h + 1, :]
        decay = jnp.exp(jnp.where(incl, gcc - gcr, NEG_INF))
        kbeta = k * beta
        kq = k.astype(BF16)
        a = jnp.where(strict, _dot_nt(kbeta.astype(BF16), kq) * decay, 0.0)
        p = eye - a
        m = a
        for _ in range(nsq):
            m = _dot_hi(m, m)
            p = p + _dot_hi(p, m)
        rhs = jnp.concatenate([v * beta, kbeta * jnp.exp(gcc)], -1)
        sol = _dot_hi(p, rhs)
        u = sol[:, :DN_HEAD_DIM]
        w = sol[:, DN_HEAD_DIM:]
        qk = _dot_nt(q.astype(BF16), kq) * decay
        s_h = s_scr[h]
        s_b = s_h.astype(BF16)
        v_new = u - _dot(w.astype(BF16), s_b)
        v_nb = v_new.astype(BF16)
        o = _dot((q * jnp.exp(gcc)).astype(BF16), s_b) + _dot(qk.astype(BF16), v_nb)
        g_last = gcr[:, c - 1:c]
        kd = k * jnp.exp(g_last - gcc)
        s_scr[h] = s_h * jnp.exp(g_last) + _dot_tn(kd.astype(BF16), v_nb)
        o = o * lax.rsqrt(jnp.mean(o * o, -1, keepdims=True) + RMS_EPS) * ng
        z = dz[:, lo:lo + DN_HEAD_DIM]
        outs.append(o * (z * _sigmoid(z)))
    o_ref[...] = jnp.concatenate(outs, -1)

    @pl.when(n == pl.num_programs(1) - 1)
    def _():
        s_ref[0] = s_scr[...]


def _softplus(x):
    return jnp.maximum(x, 0.0) + jnp.log(1.0 + jnp.exp(-jnp.abs(x)))


def _deltanet(dqkv, tail, w_conv, ba, bat, dz, s0, par_c, par_r, norm_g, nb, nchunk, c, c_real):
    nsq = max(int(math.ceil(math.log2(c))) - 1, 0)
    rows = lambda width: pl.BlockSpec((c, width), lambda b, n: (b * nchunk + n, 0))
    const2 = lambda shape: pl.BlockSpec(shape, lambda b, n: (0,) * len(shape))
    sspec = pl.BlockSpec((1, DN_HEADS, DN_HEAD_DIM, DN_HEAD_DIM), lambda b, n: (b, 0, 0, 0))
    return pl.pallas_call(
        functools.partial(_deltanet_kernel, c=c, c_real=c_real, nsq=nsq),
        grid=(nb, nchunk),
        in_specs=[
            rows(CONV_CH),
            pl.BlockSpec((SUBLANES, CONV_CH), lambda b, n: (b, 0)),
            const2((CONV_WIDTH, CONV_CH)),
            rows(LANES),
            pl.BlockSpec((1, SUBLANES, c), lambda b, n: (b * nchunk + n, 0, 0)),
            rows(DN_W),
            sspec,
            const2((SUBLANES, LANES)),
            const2((2, SUBLANES, LANES)),
            const2((1, DN_HEAD_DIM)),
        ],
        out_specs=(rows(DN_W), sspec),
        out_shape=(jax.ShapeDtypeStruct((nb * nchunk * c, DN_W), F32),
                   jax.ShapeDtypeStruct((nb, DN_HEADS, DN_HEAD_DIM, DN_HEAD_DIM), F32)),
        scratch_shapes=[pltpu.VMEM((DN_HEADS, DN_HEAD_DIM, DN_HEAD_DIM), F32),
                        pltpu.VMEM((SUBLANES + c, CONV_CH), F32)],
        compiler_params=_cparams(("parallel", "arbitrary")),
        name="deltanet_c%d" % c,
    )(dqkv, tail, w_conv, ba, bat, dz, s0, par_c, par_r, norm_g)


def _mm_kernel(x_ref, w_ref, o_ref):
    o_ref[...] = _dot(x_ref[...].astype(BF16), w_ref[...])


def _mm(x, w_b, tm):
    t, kdim = x.shape
    n = w_b.shape[1]
    return pl.pallas_call(
        _mm_kernel,
        grid=(t // tm,),
        in_specs=[pl.BlockSpec((tm, kdim), lambda i: (i, 0)),
                  pl.BlockSpec((kdim, n), lambda i: (0, 0))],
        out_specs=pl.BlockSpec((tm, n), lambda i: (i, 0)),
        out_shape=jax.ShapeDtypeStruct((t, n), F32),
        compiler_params=_cparams(("parallel",)),
        name="mm",
    )(x, w_b)


def _mm_ln_kernel(*refs, n_lhs):
    lhs = refs[:n_lhs]
    ws = refs[n_lhs:2 * n_lhs]
    res_ref, g_ref, b_ref, o_ref = refs[2 * n_lhs:]
    acc = _dot(lhs[0][...].astype(BF16), ws[0][...])
    for a_ref, w_ref in zip(lhs[1:], ws[1:]):
        acc = acc + _dot(a_ref[...].astype(BF16), w_ref[...])
    o_ref[...] = _layer_norm(DEEPNORM_ALPHA * res_ref[...] + acc, g_ref[...], b_ref[...])


def _mm_ln(lhs, ws, resid, g, b, tm):
    t = resid.shape[0]
    n_lhs = len(lhs)
    in_specs = [pl.BlockSpec((tm, a.shape[1]), lambda i: (i, 0)) for a in lhs]
    in_specs += [pl.BlockSpec(w.shape, lambda i: (0, 0)) for w in ws]
    in_specs += [pl.BlockSpec((tm, D_MODEL), lambda i: (i, 0)),
                 pl.BlockSpec((1, D_MODEL), lambda i: (0, 0)),
                 pl.BlockSpec((1, D_MODEL), lambda i: (0, 0))]
    return pl.pallas_call(
        functools.partial(_mm_ln_kernel, n_lhs=n_lhs),
        grid=(t // tm,),
        in_specs=in_specs,
        out_specs=pl.BlockSpec((tm, D_MODEL), lambda i: (i, 0)),
        out_shape=jax.ShapeDtypeStruct((t, D_MODEL), F32),
        compiler_params=_cparams(("parallel",)),
        name="mm_ln%d" % n_lhs,
    )(*lhs, *ws, resid, g, b)


def _xattn_kernel(q_ref, k_ref, v_ref, o_ref):
    q = q_ref[...].astype(BF16)
    k = k_ref[...].astype(BF16)
    v = v_ref[...].astype(BF16)
    outs = []
    for h in range(X_HEADS):
        lo = h * X_HEAD_DIM
        s = _dot_nt(q[:, lo:lo + X_HEAD_DIM], k[:, lo:lo + X_HEAD_DIM]) * (X_HEAD_DIM ** -0.5)
        m = jnp.max(s, -1, keepdims=True)
        p = jnp.exp(s - m)
        p = p * (1.0 / jnp.sum(p, -1, keepdims=True))
        outs.append(_dot(p.astype(BF16), v[:, lo:lo + X_HEAD_DIM]))
    o_ref[...] = jnp.concatenate(outs, -1)


def _xattn(q, mk, mv, nb, nblk, tq):
    qspec = pl.BlockSpec((tq, D_MODEL), lambda b, i: (b * nblk + i, 0))
    mspec = pl.BlockSpec((MEM_LEN, D_MODEL), lambda b, i: (b, 0))
    return pl.pallas_call(
        _xattn_kernel,
        grid=(nb, nblk),
        in_specs=[qspec, mspec, mspec],
        out_specs=qspec,
        out_shape=jax.ShapeDtypeStruct(q.shape, F32),
        compiler_params=_cparams(("parallel", "arbitrary")),
        name="xattn",
    )(q, mk, mv)


def _topk_rows(s, kk):
    n = s.shape[0]
    ridx = lax.broadcasted_iota(jnp.int32, s.shape, 0).astype(F32)
    vals, idxs = [], []
    for _ in range(kk):
        m = jnp.max(s, axis=0, keepdims=True)
        idx = jnp.min(jnp.where(s == m, ridx, float(n)), axis=0, keepdims=True)
        vals.append(m)
        idxs.append(idx)
        s = jnp.where(ridx == idx, -jnp.inf, s)
    return jnp.concatenate(vals, 0), jnp.concatenate(idxs, 0)


def _peer_sel_kernel(x_ref, wq_ref, k1_ref, k2_ref, i1_ref, i2_ref, g_ref, q_scr):
    h = pl.program_id(1)
    half = PEER_QDIM // 2

    @pl.when(h == 0)
    def _():
        q = _dot(x_ref[...].astype(BF16), wq_ref[...])
        for hh in range(PEER_HEADS):
            q_scr[hh] = q[:, hh * PEER_QDIM:(hh + 1) * PEER_QDIM]

    q = q_scr[h]
    s1 = _dot_nt(k1_ref[...], q[:, :half].astype(BF16))
    s2 = _dot_nt(k2_ref[...], q[:, half:].astype(BF16))
    v1, i1 = _topk_rows(s1, PEER_TOPK)
    v2, i2 = _topk_rows(s2, PEER_TOPK)
    cand = jnp.concatenate([v1[a:a + 1] + v2 for a in range(PEER_TOPK)], 0)
    sc, ci = _topk_rows(cand, PEER_TOPK)
    ca = jnp.floor(ci * (1.0 / PEER_TOPK))
    cb = ci - ca * PEER_TOPK
    e1 = jnp.zeros_like(ci)
    e2 = jnp.zeros_like(ci)
    for a in range(PEER_TOPK):
        e1 = jnp.where(ca == float(a), i1[a:a + 1], e1)
        e2 = jnp.where(cb == float(a), i2[a:a + 1], e2)
    ex = jnp.exp(sc - sc[0:1])
    gate = ex * (1.0 / jnp.sum(ex, axis=0, keepdims=True))
    i1_ref[0] = e1
    i2_ref[0] = e2
    g_ref[0] = gate


def _peer_sel(x, wq_b, k1_b, k2_b):
    t = x.shape[0]
    tb = LANES
    nblk = t // tb
    ospec = pl.BlockSpec((1, PEER_TOPK, tb), lambda i, h: (i, h, 0))
    oshape = jax.ShapeDtypeStruct((nblk, PEER_HEADS * PEER_TOPK, tb), F32)
    return pl.pallas_call(
        _peer_sel_kernel,
        grid=(nblk, PEER_HEADS),
        in_specs=[pl.BlockSpec((tb, D_MODEL), lambda i, h: (i, 0)),
                  pl.BlockSpec((D_MODEL, PEER_HEADS * PEER_QDIM), lambda i, h: (0, 0)),
                  pl.BlockSpec((N_KEYS, PEER_QDIM // 2), lambda i, h: (0, 0)),
                  pl.BlockSpec((N_KEYS, PEER_QDIM // 2), lambda i, h: (0, 0))],
        out_specs=(ospec, ospec, ospec),
        out_shape=(oshape, oshape, oshape),
        scratch_shapes=[pltpu.VMEM((PEER_HEADS, tb, PEER_QDIM), F32)],
        compiler_params=_cparams(("parallel", "arbitrary")),
        name="peer_sel",
    )(x, wq_b, k1_b, k2_b)


G_PITCH_PAD = 8


def _peer_g_kernel(i1_ref, i2_ref, g_ref, o_ref, i1_scr, i2_scr, g_scr, buf, *, tb):
    pitch = tb + G_PITCH_PAD
    nsel = PEER_HEADS * PEER_TOPK
    i1_scr[...] = i1_ref[0].T
    i2_scr[...] = i2_ref[0].T
    g_scr[...] = g_ref[0].T
    kidx = lax.broadcasted_iota(jnp.int32, (N_KEYS, nsel), 0).astype(F32)

    def body(t, carry):
        r1 = i1_scr[pl.ds(t, 1), :]
        r2 = i2_scr[pl.ds(t, 1), :]
        rg = g_scr[pl.ds(t, 1), :]
        at = jnp.where(kidx == r1, rg, 0.0).astype(BF16)
        bt = jnp.where(kidx == r2, 1.0, 0.0).astype(BF16)
        gt = _dot_nt(at, bt)
        for r in range(N_KEYS // SUBLANES):
            buf[pl.ds(r * SUBLANES * pitch + t, SUBLANES, stride=pitch), :] = (
                gt[r * SUBLANES:(r + 1) * SUBLANES, :])
        return carry

    lax.fori_loop(0, tb, body, 0)
    for i1 in range(N_KEYS):
        o_ref[i1] = buf[i1 * pitch:i1 * pitch + tb, :].astype(BF16)


def _peer_g(i1, i2, g):
    nblk, nsel, tb = i1.shape
    ispec = pl.BlockSpec((1, nsel, tb), lambda i: (i, 0, 0))
    return pl.pallas_call(
        functools.partial(_peer_g_kernel, tb=tb),
        grid=(nblk,),
        in_specs=[ispec, ispec, ispec],
        out_specs=pl.BlockSpec((N_KEYS, tb, N_KEYS), lambda i: (0, i, 0)),
        out_shape=jax.ShapeDtypeStruct((N_KEYS, nblk * tb, N_KEYS), BF16),
        scratch_shapes=[pltpu.VMEM((tb, nsel), F32), pltpu.VMEM((tb, nsel), F32),
                        pltpu.VMEM((tb, nsel), F32),
                        pltpu.VMEM((N_KEYS * (tb + G_PITCH_PAD), N_KEYS), F32)],
        compiler_params=_cparams(("parallel",)),
        name="peer_g",
    )(i1, i2, g)


PEER_I1_PER_CHUNK = 8
PEER_CHUNK = PEER_I1_PER_CHUNK * N_KEYS


def _gelu_tanh(x):
    return 0.5 * x * (1.0 + jnp.tanh(math.sqrt(2.0 / math.pi) * (x + 0.044715 * (x * x * x))))


def _peer_ffn_kernel(x_ref, u_ref, v_ref, gm_ref, lg_ref, lb_ref, o_ref, xb_scr, acc_scr):
    c = pl.program_id(1)

    @pl.when(c == 0)
    def _():
        xb_scr[...] = x_ref[...].astype(BF16)
        acc_scr[...] = jnp.zeros_like(acc_scr)

    hid = _dot_nt(xb_scr[...], u_ref[...])
    ws = []
    for j in range(PEER_I1_PER_CHUNK):
        hj = hid[:, j * N_KEYS:(j + 1) * N_KEYS]
        ws.append((_gelu_tanh(hj) * gm_ref[j].astype(F32)).astype(BF16))
    acc_scr[...] += _dot(jnp.concatenate(ws, -1), v_ref[...])

    @pl.when(c == pl.num_programs(1) - 1)
    def _():
        o_ref[...] = _layer_norm(DEEPNORM_ALPHA * x_ref[...] + acc_scr[...], lg_ref[...], lb_ref[...])


def _peer_ffn(x, u_b, v_b, gmap, lg, lb, tm):
    t = x.shape[0]
    nchunk = u_b.shape[0] // PEER_CHUNK
    return pl.pallas_call(
        _peer_ffn_kernel,
        grid=(t // tm, nchunk),
        in_specs=[pl.BlockSpec((tm, D_MODEL), lambda i, c: (i, 0)),
                  pl.BlockSpec((PEER_CHUNK, D_MODEL), lambda i, c: (c, 0)),
                  pl.BlockSpec((PEER_CHUNK, D_MODEL), lambda i, c: (c, 0)),
                  pl.BlockSpec((PEER_I1_PER_CHUNK, tm, N_KEYS), lambda i, c: (c, i, 0)),
                  pl.BlockSpec((1, D_MODEL), lambda i, c: (0, 0)),
                  pl.BlockSpec((1, D_MODEL), lambda i, c: (0, 0))],
        out_specs=pl.BlockSpec((tm, D_MODEL), lambda i, c: (i, 0)),
        out_shape=jax.ShapeDtypeStruct((t, D_MODEL), F32),
        scratch_shapes=[pltpu.VMEM((tm, D_MODEL), BF16), pltpu.VMEM((tm, D_MODEL), F32)],
        compiler_params=_cparams(("parallel", "arbitrary")),
        name="peer_ffn",
    )(x, u_b, v_b, gmap, lg, lb)


def _peer(x2, wts, tm):
    i1, i2, g = _peer_sel(x2, wts["w_pq"], wts["keys1"], wts["keys2"])
    gmap = _peer_g(i1, i2, g)
    return _peer_ffn(x2, wts["peer_u"], wts["peer_v"], gmap, wts["ln3_g"], wts["ln3_b"], tm)


def _rope_tables(pos):
    half = ATT_HEAD_DIM // 2
    inv_freq = ROPE_THETA ** (-jnp.arange(half, dtype=F32) / half)
    ang = pos.astype(F32)[:, None] * inv_freq[None, :]
    cos = jnp.cos(ang)
    sin = jnp.sin(ang)
    reps = LANES // ATT_HEAD_DIM
    cos_t = jnp.tile(jnp.concatenate([cos, cos], -1), (1, reps))
    sin_t = jnp.tile(jnp.concatenate([-sin, sin], -1), (1, reps))
    return cos_t, sin_t


def _prep_weights(w_in, w_conv, dn_a_log, dn_dt_bias, dn_norm_g, attn_sinks, w_o, ln1_g, ln1_b,
                  w_cq, w_ck, w_cv, w_co, ln2_g, ln2_b, w_pq, peer_keys1, peer_keys2,
                  peer_u, peer_v, ln3_g, ln3_b):
    o_dqkv = ATT_Q + 2 * ATT_KV
    o_ba = o_dqkv + CONV_CH
    o_dz = o_ba + 2 * DN_HEADS
    w_ba = w_in[:, o_ba:o_dz]
    w_p = jnp.concatenate([w_in[:, :o_ba], w_in[:, o_dz:], w_ba,
                           jnp.zeros((D_MODEL, LANES - 2 * DN_HEADS), F32)], -1).astype(BF16)
    lane_par = lambda p: jnp.zeros((LANES,), F32).at[DN_HEADS:2 * DN_HEADS].set(p)
    par_c = jnp.zeros((SUBLANES, LANES), F32).at[0].set(lane_par(dn_a_log)).at[1].set(lane_par(dn_dt_bias))
    row_par = lambda p: jnp.broadcast_to(
        jnp.zeros((SUBLANES,), F32).at[DN_HEADS:2 * DN_HEADS].set(p)[:, None], (SUBLANES, LANES))
    par_r = jnp.stack([row_par(dn_a_log), row_par(dn_dt_bias)])
    row = lambda p: p.reshape(1, -1)
    return dict(
        w_in=w_p, w_ba_t=w_ba.T.astype(BF16), w_conv=w_conv, par_c=par_c, par_r=par_r,
        norm_g=row(dn_norm_g), sinks=attn_sinks,
        w_o_a=w_o[:ATT_Q].astype(BF16), w_o_d=w_o[ATT_Q:].astype(BF16),
        ln1_g=row(ln1_g), ln1_b=row(ln1_b),
        w_cq=w_cq.astype(BF16), w_ck=w_ck.astype(BF16), w_cv=w_cv.astype(BF16), w_co=w_co.astype(BF16),
        ln2_g=row(ln2_g), ln2_b=row(ln2_b),
        w_pq=w_pq.reshape(D_MODEL, PEER_HEADS * PEER_QDIM).astype(BF16),
        keys1=peer_keys1.astype(BF16), keys2=peer_keys2.astype(BF16),
        peer_u=peer_u.astype(BF16), peer_v=peer_v.astype(BF16),
        ln3_g=row(ln3_g), ln3_b=row(ln3_b),
    )


def _tile(t, pref):
    while t % pref:
        pref //= 2
    return pref


def _layer_tail(xf, a_out, dn_out, mk, mv, nb, seq, wts):
    t = xf.shape[0]
    tm = _tile(t, 256)
    x1 = _mm_ln([a_out, dn_out], [wts["w_o_a"], wts["w_o_d"]], xf, wts["ln1_g"], wts["ln1_b"], tm)
    qc = _mm(x1, wts["w_cq"], tm)
    tq = _tile(seq, 512)
    xo = _xattn(qc, mk, mv, nb, seq // tq, tq)
    x2 = _mm_ln([xo], [wts["w_co"]], x1, wts["ln2_g"], wts["ln2_b"], tm)
    return _peer(x2, wts, _tile(t, 512))


def _prompt_layer(x, mem, wts):
    nb, seq, _ = x.shape
    t = nb * seq
    xf = x.reshape(t, D_MODEL)
    cos, sin = _rope_tables(jnp.arange(seq, dtype=jnp.int32))
    q, k, v, dqkv, dz, ba, bat = _in_proj(xf, wts["w_in"], wts["w_ba_t"], cos, sin, _tile(seq, 256))
    nblk = seq // WINDOW
    a_out = _swa(wts["sinks"], q, k, v, nb, nblk, WINDOW, WINDOW, True)
    c = DN_CHUNK if seq % DN_CHUNK == 0 else seq
    nchunk = seq // c
    bat_c = bat.reshape(SUBLANES, nb * nchunk, c).transpose(1, 0, 2)
    tail = jnp.zeros((nb * SUBLANES, CONV_CH), F32)
    s0 = jnp.zeros((nb, DN_HEADS, DN_HEAD_DIM, DN_HEAD_DIM), F32)
    dn_out, s_new = _deltanet(dqkv, tail, wts["w_conv"], ba, bat_c, dz, s0, wts["par_c"], wts["par_r"],
                              wts["norm_g"], nb, nchunk, c, c)
    memf = mem.reshape(nb * MEM_LEN, D_MODEL)
    mk = _mm(memf, wts["w_ck"], _tile(nb * MEM_LEN, 256))
    mv = _mm(memf, wts["w_cv"], _tile(nb * MEM_LEN, 256))
    y = _layer_tail(xf, a_out, dn_out, mk, mv, nb, seq, wts)
    new_k = k.reshape(nb, seq, ATT_KV_HEADS, ATT_HEAD_DIM)[:, -WINDOW:]
    new_v = v.reshape(nb, seq, ATT_KV_HEADS, ATT_HEAD_DIM)[:, -WINDOW:]
    new_conv = dqkv.reshape(nb, seq, CONV_CH)[:, -(CONV_WIDTH - 1):]
    mk4 = mk.reshape(nb, MEM_LEN, X_HEADS, X_HEAD_DIM)
    mv4 = mv.reshape(nb, MEM_LEN, X_HEADS, X_HEAD_DIM)
    return y.reshape(nb, seq, D_MODEL), new_k, new_v, s_new, new_conv, mk4, mv4


def _sample_layer(x, past_k, past_v, s0, conv_buf, mem_k, mem_v, wts):
    nb, seq, _ = x.shape
    assert CONV_WIDTH - 1 <= seq <= SAMPLE_PAD
    pad = SAMPLE_PAD
    t = nb * pad
    xf = jnp.pad(x, ((0, 0), (0, pad - seq), (0, 0))).reshape(t, D_MODEL)
    cos8, sin8 = _rope_tables(PAST_LEN + jnp.arange(pad, dtype=jnp.int32))
    tm = _tile(t, 256)
    cos = jnp.tile(cos8, (tm // pad, 1))
    sin = jnp.tile(sin8, (tm // pad, 1))
    q, k, v, dqkv, dz, ba, bat = _in_proj(xf, wts["w_in"], wts["w_ba_t"], cos, sin, tm)
    npast = past_k.shape[1]
    kk = jnp.concatenate([past_k.reshape(nb, npast, ATT_KV), k.reshape(nb, pad, ATT_KV)], 1)
    vv = jnp.concatenate([past_v.reshape(nb, npast, ATT_KV), v.reshape(nb, pad, ATT_KV)], 1)
    nk = npast + pad
    a_out = _swa(wts["sinks"], q, kk.reshape(nb * nk, ATT_KV), vv.reshape(nb * nk, ATT_KV),
                 nb, 1, pad, nk, False)
    bat_c = bat.reshape(SUBLANES, nb, pad).transpose(1, 0, 2)
    tail = jnp.pad(conv_buf, ((0, 0), (SUBLANES - (CONV_WIDTH - 1), 0), (0, 0))).reshape(nb * SUBLANES, CONV_CH)
    dn_out, s_new = _deltanet(dqkv, tail, wts["w_conv"], ba, bat_c, dz, s0, wts["par_c"], wts["par_r"],
                              wts["norm_g"], nb, 1, pad, seq)
    mk = mem_k.reshape(nb * MEM_LEN, D_MODEL)
    mv = mem_v.reshape(nb * MEM_LEN, D_MODEL)
    y = _layer_tail(xf, a_out, dn_out, mk, mv, nb, pad, wts)
    new_k = kk[:, seq:seq + npast].reshape(nb, npast, ATT_KV_HEADS, ATT_HEAD_DIM)
    new_v = vv[:, seq:seq + npast].reshape(nb, npast, ATT_KV_HEADS, ATT_HEAD_DIM)
    new_conv = dqkv.reshape(nb, pad, CONV_CH)[:, seq - (CONV_WIDTH - 1):seq]
    return y.reshape(nb, pad, D_MODEL)[:, :seq], new_k, new_v, s_new, new_conv


def kernel(x_prompt, x_sample, cache_swa_k, cache_swa_v, state_dn, state_dn_conv, cache_mem_k, cache_mem_v, mem_prompt, w_in, w_conv, dn_a_log, dn_dt_bias, dn_norm_g, attn_sinks, w_o, ln1_g, ln1_b, w_cq, w_ck, w_cv, w_co, ln2_g, ln2_b, w_pq, peer_keys1, peer_keys2, peer_u, peer_v, ln3_g, ln3_b):
    assert w_in.shape[0] == DEPTH == 1
    wts = _prep_weights(w_in[0], w_conv[0], dn_a_log[0], dn_dt_bias[0], dn_norm_g[0], attn_sinks[0],
                        w_o[0], ln1_g[0], ln1_b[0], w_cq[0], w_ck[0], w_cv[0], w_co[0], ln2_g[0], ln2_b[0],
                        w_pq[0], peer_keys1[0], peer_keys2[0], peer_u[0], peer_v[0], ln3_g[0], ln3_b[0])
    y_p, pk, pv, ps, pc, pmk, pmv = _prompt_layer(x_prompt, mem_prompt, wts)
    y_s, sk, sv, ss, sc = _sample_layer(x_sample, cache_swa_k[0], cache_swa_v[0], state_dn[0],
                                        state_dn_conv[0], cache_mem_k[0], cache_mem_v[0], wts)
    st = lambda a: a[None]
    return (y_p, y_s, st(pk), st(pv), st(ps), st(pc), st(pmk), st(pmv), st(sk), st(sv), st(ss), st(sc))
```

```python
import functools
import math

import jax
import jax.numpy as jnp
from jax import lax
from jax.experimental import pallas as pl
from jax.experimental.pallas import tpu as pltpu

F32 = jnp.float32
BF16 = jnp.bfloat16

D_MODEL = 1024
DEPTH = 1
PAST_LEN = 16384
ATT_HEADS = 8
ATT_KV_HEADS = 2
ATT_HEAD_DIM = 64
ATT_GROUP = ATT_HEADS // ATT_KV_HEADS
WINDOW = 128
ROPE_THETA = 10000.0
DN_HEADS = 4
DN_HEAD_DIM = 128
CONV_WIDTH = 4
DN_CHUNK = 64
MEM_LEN = 256
X_HEADS = 4
X_HEAD_DIM = D_MODEL // X_HEADS
N_KEYS = 128
PEER_HEADS = 8
PEER_QDIM = 256
PEER_TOPK = 16
LN_EPS = 1e-5
RMS_EPS = 1e-6
NEG_INF = -1e30
DEEPNORM_ALPHA = (2 * DEPTH) ** 0.25

ATT_Q = ATT_HEADS * ATT_HEAD_DIM
ATT_KV = ATT_KV_HEADS * ATT_HEAD_DIM
DN_W = DN_HEADS * DN_HEAD_DIM
CONV_CH = 3 * DN_W
IN_PACKED = ATT_Q + 2 * ATT_KV + CONV_CH + DN_W + 128

LANES = 128
SUBLANES = 8
VMEM_LIMIT = 48 * 1024 * 1024
SAMPLE_PAD = 8


def _cparams(sem):
    return pltpu.CompilerParams(dimension_semantics=sem, vmem_limit_bytes=VMEM_LIMIT)


def _dot(a, b):
    return jnp.dot(a, b, preferred_element_type=F32)


def _dot_nt(a, b):
    return lax.dot_general(a, b, (((1,), (1,)), ((), ())), preferred_element_type=F32)


def _dot_tn(a, b):
    return lax.dot_general(a, b, (((0,), (0,)), ((), ())), preferred_element_type=F32)


def _split3(x):
    x0 = x.astype(BF16)
    r = x - x0.astype(F32)
    x1 = r.astype(BF16)
    x2 = (r - x1.astype(F32)).astype(BF16)
    return x0, x1, x2


def _dot_exact_lhs(a_bf16, b):
    b0, b1, b2 = _split3(b)
    return _dot(a_bf16, b0) + _dot(a_bf16, b1) + _dot(a_bf16, b2)


def _dot_exact_rhs(a, b_bf16):
    a0, a1, a2 = _split3(a)
    return _dot(a0, b_bf16) + _dot(a1, b_bf16) + _dot(a2, b_bf16)


def _dot_hi(a, b):
    a0 = a.astype(BF16)
    a1 = (a - a0.astype(F32)).astype(BF16)
    b0 = b.astype(BF16)
    b1 = (b - b0.astype(F32)).astype(BF16)
    return _dot(a0, b0) + _dot(a0, b1) + _dot(a1, b0)


def _sigmoid(x):
    return 1.0 / (1.0 + jnp.exp(-x))


def _layer_norm(y, g, b):
    mu = jnp.mean(y, -1, keepdims=True)
    d = y - mu
    var = jnp.mean(d * d, -1, keepdims=True)
    return d * lax.rsqrt(var + LN_EPS) * g + b


def _in_proj_kernel(x_ref, w_ref, wt_ref, cos_ref, sin_ref,
                    q_ref, k_ref, v_ref, dqkv_ref, dz_ref, ba_ref, bat_ref):
    xb = x_ref[...].astype(BF16)
    h = _dot(xb, w_ref[...])
    cos = cos_ref[...]
    sin = sin_ref[...]
    lane = lax.broadcasted_iota(jnp.int32, cos.shape, 1)
    first = (lane & (ATT_HEAD_DIM - 1)) < (ATT_HEAD_DIM // 2)

    def rope(t):
        up = pltpu.roll(t, LANES - ATT_HEAD_DIM // 2, 1)
        dn = pltpu.roll(t, ATT_HEAD_DIM // 2, 1)
        return t * cos + jnp.where(first, up, dn) * sin

    for j in range(ATT_Q // LANES):
        q_ref[:, j * LANES:(j + 1) * LANES] = rope(h[:, j * LANES:(j + 1) * LANES])
    k_ref[...] = rope(h[:, ATT_Q:ATT_Q + ATT_KV])
    o = ATT_Q + ATT_KV
    v_ref[...] = h[:, o:o + ATT_KV]
    o += ATT_KV
    dqkv_ref[...] = h[:, o:o + CONV_CH]
    o += CONV_CH
    dz_ref[...] = h[:, o:o + DN_W]
    o += DN_W
    ba_ref[...] = h[:, o:o + LANES]
    bat_ref[...] = _dot_nt(wt_ref[...], xb)


def _in_proj(x, w_p, w_t, cos, sin, tm):
    t = x.shape[0]
    nt = t // tm
    ncos = cos.shape[0] // tm
    row = lambda width: pl.BlockSpec((tm, width), lambda i: (i, 0))
    out_shape = (
        jax.ShapeDtypeStruct((t, ATT_Q), F32),
        jax.ShapeDtypeStruct((t, ATT_KV), F32),
        jax.ShapeDtypeStruct((t, ATT_KV), F32),
        jax.ShapeDtypeStruct((t, CONV_CH), F32),
        jax.ShapeDtypeStruct((t, DN_W), F32),
        jax.ShapeDtypeStruct((t, LANES), F32),
        jax.ShapeDtypeStruct((SUBLANES, t), F32),
    )
    return pl.pallas_call(
        _in_proj_kernel,
        grid=(nt,),
        in_specs=[
            row(D_MODEL),
            pl.BlockSpec((D_MODEL, IN_PACKED), lambda i: (0, 0)),
            pl.BlockSpec((SUBLANES, D_MODEL), lambda i: (0, 0)),
            pl.BlockSpec((tm, LANES), lambda i: (i % ncos, 0)),
            pl.BlockSpec((tm, LANES), lambda i: (i % ncos, 0)),
        ],
        out_specs=(row(ATT_Q), row(ATT_KV), row(ATT_KV), row(CONV_CH), row(DN_W), row(LANES),
                   pl.BlockSpec((SUBLANES, tm), lambda i: (0, i))),
        out_shape=out_shape,
        compiler_params=_cparams(("parallel",)),
        name="in_proj",
    )(x, w_p, w_t, cos, sin)


def _swa_kernel(sinks_ref, q_ref, *refs, has_prev):
    if has_prev:
        kp_ref, kc_ref, vp_ref, vc_ref, o_ref = refs
        k = jnp.concatenate([kp_ref[...], kc_ref[...]], 0)
        v = jnp.concatenate([vp_ref[...], vc_ref[...]], 0)
    else:
        kc_ref, vc_ref, o_ref = refs
        k = kc_ref[...]
        v = vc_ref[...]
    q = q_ref[...]
    r, nk = q.shape[0], k.shape[0]
    qi = lax.broadcasted_iota(jnp.int32, (r, nk), 0)
    kj = lax.broadcasted_iota(jnp.int32, (r, nk), 1)
    valid = (kj >= qi) & (kj <= qi + WINDOW)
    if has_prev:
        jmin = jnp.where(pl.program_id(1) == 0, WINDOW, 0)
        valid = valid & (kj >= jmin)
    kb = k.astype(BF16)
    vb = v.astype(BF16)
    outs = []
    for h in range(ATT_HEADS):
        kvh = h // ATT_GROUP
        qh = q[:, h * ATT_HEAD_DIM:(h + 1) * ATT_HEAD_DIM].astype(BF16)
        kh = kb[:, kvh * ATT_HEAD_DIM:(kvh + 1) * ATT_HEAD_DIM]
        s = _dot_nt(qh, kh) * (ATT_HEAD_DIM ** -0.5)
        s = jnp.where(valid, s, NEG_INF)
        sink = sinks_ref[h]
        m = jnp.maximum(jnp.max(s, -1, keepdims=True), sink)
        p = jnp.exp(s - m)
        den = jnp.sum(p, -1, keepdims=True) + jnp.exp(sink - m)
        p = p * (1.0 / den)
        outs.append(_dot(p.astype(BF16), vb[:, kvh * ATT_HEAD_DIM:(kvh + 1) * ATT_HEAD_DIM]))
    o_ref[...] = jnp.concatenate(outs, -1)


def _swa(sinks, q, k, v, nb, nblk, rq, nk, has_prev):
    smem = pl.BlockSpec(memory_space=pltpu.SMEM)
    qspec = pl.BlockSpec((rq, ATT_Q), lambda b, n: (b * nblk + n, 0))
    cur = pl.BlockSpec((nk, ATT_KV), lambda b, n: (b * nblk + n, 0))
    if has_prev:
        prev = pl.BlockSpec((nk, ATT_KV), lambda b, n: (b * nblk + jnp.maximum(n - 1, 0), 0))
        in_specs = [smem, qspec, prev, cur, prev, cur]
        args = (sinks, q, k, k, v, v)
    else:
        in_specs = [smem, qspec, cur, cur]
        args = (sinks, q, k, v)
    return pl.pallas_call(
        functools.partial(_swa_kernel, has_prev=has_prev),
        grid=(nb, nblk),
        in_specs=in_specs,
        out_specs=qspec,
        out_shape=jax.ShapeDtypeStruct(q.shape, F32),
        compiler_params=_cparams(("parallel", "arbitrary")),
        name="swa_prev" if has_prev else "swa_cache",
    )(*args)


def _deltanet_kernel(x_ref, tail_ref, wc_ref, ba_ref, bat_ref, dz_ref, s0_ref,
                     pc_ref, pr_ref, ng_ref, o_ref, s_ref, s_scr, xp_scr, *, c, c_real, nsq, group, chained):
    n = pl.program_id(1)
    rows = group * c
    wc = wc_ref[...]
    base = SUBLANES - (CONV_WIDTH - 1)

    def conv(lo, nrow):
        acc = xp_scr[lo + base:lo + base + nrow] * wc[0:1]
        for j in range(1, CONV_WIDTH):
            acc = acc + xp_scr[lo + base + j:lo + base + j + nrow] * wc[j:j + 1]
        return acc

    if chained:
        @pl.when(n == 0)
        def _():
            s_scr[...] = s0_ref[0]
            xp_scr[0:SUBLANES] = tail_ref[...]

        xp_scr[SUBLANES:SUBLANES + rows] = x_ref[...]
        y = conv(0, rows)
        xp_scr[0:SUBLANES] = xp_scr[rows:rows + SUBLANES]
    else:
        seg = SUBLANES + c
        ys = []
        for u in range(group):
            xp_scr[u * seg:u * seg + SUBLANES] = tail_ref[u * SUBLANES:(u + 1) * SUBLANES]
            xp_scr[u * seg + SUBLANES:(u + 1) * seg] = x_ref[u * c:(u + 1) * c]
            ys.append(conv(u * seg, c))
        y = jnp.concatenate(ys, 0)
    y = y * _sigmoid(y)

    ba = ba_ref[...]
    bat = jnp.concatenate([bat_ref[u] for u in range(group)], -1)
    beta_c = _sigmoid(ba)
    g_c = -jnp.exp(pc_ref[0:1]) * _softplus(ba + pc_ref[1:2])
    beta_r = _sigmoid(bat)
    g_r = -jnp.exp(pr_ref[0][:, 0:1]) * _softplus(bat + pr_ref[1][:, 0:1])
    if c_real < c:
        rmask = (lax.broadcasted_iota(jnp.int32, ba.shape, 0) & (c - 1)) < c_real
        beta_c = jnp.where(rmask, beta_c, 0.0)
        g_c = jnp.where(rmask, g_c, 0.0)
        lmask = (lax.broadcasted_iota(jnp.int32, bat.shape, 1) & (c - 1)) < c_real
        beta_r = jnp.where(lmask, beta_r, 0.0)
        g_r = jnp.where(lmask, g_r, 0.0)

    shift = int(math.log2(c))
    bi = lax.broadcasted_iota(jnp.int32, (rows, rows), 0)
    bj = lax.broadcasted_iota(jnp.int32, (rows, rows), 1)
    same = (bi >> shift) == (bj >> shift)
    tri = jnp.where(same & (bi >= bj), 1.0, 0.0).astype(BF16)
    tri_t = jnp.where(same & (bi <= bj), 1.0, 0.0).astype(BF16)
    gc_c = _dot_exact_lhs(tri, g_c)
    gc_r = _dot_exact_rhs(g_r, tri_t)
    ii = lax.broadcasted_iota(jnp.int32, (c, c), 0)
    jj = lax.broadcasted_iota(jnp.int32, (c, c), 1)
    eye = jnp.where(ii == jj, 1.0, 0.0)
    incl = ii >= jj
    strict = ii > jj

    ng = ng_ref[...]
    dz = dz_ref[...]
    hs = range(DN_HEADS)
    uh = [(u, h) for u in range(group) for h in hs]
    rs = lambda u: slice(u * c, (u + 1) * c)
    sl = lambda base, h: slice(base + h * DN_HEAD_DIM, base + (h + 1) * DN_HEAD_DIM)
    l2 = lambda t: t * lax.rsqrt(jnp.sum(t * t, -1, keepdims=True) + RMS_EPS)
    q = {x: l2(y[rs(x[0]), sl(0, x[1])]) * (DN_HEAD_DIM ** -0.5) for x in uh}
    k = {x: l2(y[rs(x[0]), sl(DN_W, x[1])]) for x in uh}
    v = {x: y[rs(x[0]), sl(2 * DN_W, x[1])] for x in uh}
    beta = {x: beta_c[rs(x[0]), x[1]:x[1] + 1] for x in uh}
    gcc = {x: gc_c[rs(x[0]), DN_HEADS + x[1]:DN_HEADS + x[1] + 1] for x in uh}
    gcr = {x: gc_r[DN_HEADS + x[1]:DN_HEADS + x[1] + 1, rs(x[0])] for x in uh}
    decay = {x: jnp.exp(jnp.where(incl, gcc[x] - gcr[x], NEG_INF)) for x in uh}
    kbeta = {x: k[x] * beta[x] for x in uh}
    kq = {x: k[x].astype(BF16) for x in uh}
    a = {x: jnp.where(strict, _dot_nt(kbeta[x].astype(BF16), kq[x]) * decay[x], 0.0) for x in uh}
    qk = {x: (_dot_nt(q[x].astype(BF16), kq[x]) * decay[x]).astype(BF16) for x in uh}
    p = {x: eye - a[x] for x in uh}
    m = a
    for _ in range(nsq):
        m = {x: _dot_hi(m[x], m[x]) for x in uh}
        p = {x: p[x] + _dot_hi(p[x], m[x]) for x in uh}
    sol = {x: _dot_hi(p[x], jnp.concatenate([v[x] * beta[x], kbeta[x] * jnp.exp(gcc[x])], -1)) for x in uh}
    g_last = {x: gcr[x][:, c - 1:c] for x in uh}
    qg = {x: (q[x] * jnp.exp(gcc[x])).astype(BF16) for x in uh}
    kd = {x: (k[x] * jnp.exp(g_last[x] - gcc[x])).astype(BF16) for x in uh}

    def advance(xs, s_old):
        s_b = {x: s_old[x].astype(BF16) for x in xs}
        v_nb = {x: (sol[x][:, :DN_HEAD_DIM] - _dot(sol[x][:, DN_HEAD_DIM:].astype(BF16), s_b[x])).astype(BF16)
                for x in xs}
        s_new = {x: s_old[x] * jnp.exp(g_last[x]) + _dot_tn(kd[x], v_nb[x]) for x in xs}
        o = {x: _dot(qg[x], s_b[x]) + _dot(qk[x], v_nb[x]) for x in xs}
        return s_new, o

    o_all = {}
    if chained:
        s_cur = {h: s_scr[h] for h in hs}
        for u in range(group):
            xs = [(u, h) for h in hs]
            s_new, o = advance(xs, {x: s_cur[x[1]] for x in xs})
            s_cur = {h: s_new[(u, h)] for h in hs}
            o_all.update(o)
        for h in hs:
            s_scr[h] = s_cur[h]
    else:
        s_new, o_all = advance(uh, {x: s0_ref[x[0], x[1]] for x in uh})
        for x in uh:
            s_ref[x[0], x[1]] = s_new[x]

    outs = []
    for u in range(group):
        parts = []
        for h in hs:
            o = o_all[(u, h)]
            o = o * lax.rsqrt(jnp.mean(o * o, -1, keepdims=True) + RMS_EPS) * ng
            z = dz[rs(u), sl(0, h)]
            parts.append(o * (z * _sigmoid(z)))
        outs.append(jnp.concatenate(parts, -1))
    o_ref[...] = jnp.concatenate(outs, 0)

    if chained:
        @pl.when(n == pl.num_programs(1) - 1)
        def _():
            s_ref[0] = s_scr[...]


def _softplus(x):
    return jnp.maximum(x, 0.0) + jnp.log(1.0 + jnp.exp(-jnp.abs(x)))


DN_GROUP_CHUNKS = 4
DN_GROUP_SEQS = 8


def _deltanet(dqkv, tail, w_conv, ba, bat, dz, s0, par_c, par_r, norm_g, nb, nchunk, c, c_real):
    assert c & (c - 1) == 0
    nsq = max(int(math.ceil(math.log2(c))) - 1, 0)
    chained = nchunk > 1
    group = _tile(nchunk, DN_GROUP_CHUNKS) if chained else _tile(nb, DN_GROUP_SEQS)
    nstep = nchunk // group if chained else 1
    ngrid0 = nb if chained else nb // group
    nstate = 1 if chained else group
    ntail = SUBLANES if chained else group * SUBLANES
    rows = lambda width: pl.BlockSpec((group * c, width), lambda b, n: (b * nstep + n, 0))
    const2 = lambda shape: pl.BlockSpec(shape, lambda b, n: (0,) * len(shape))
    sspec = pl.BlockSpec((nstate, DN_HEADS, DN_HEAD_DIM, DN_HEAD_DIM), lambda b, n: (b, 0, 0, 0))
    xp_rows = SUBLANES + group * c if chained else group * (SUBLANES + c)
    return pl.pallas_call(
        functools.partial(_deltanet_kernel, c=c, c_real=c_real, nsq=nsq, group=group, chained=chained),
        grid=(ngrid0, nstep),
        in_specs=[
            rows(CONV_CH),
            pl.BlockSpec((ntail, CONV_CH), lambda b, n: (b, 0)),
            const2((CONV_WIDTH, CONV_CH)),
            rows(LANES),
            pl.BlockSpec((group, SUBLANES, c), lambda b, n: (b * nstep + n, 0, 0)),
            rows(DN_W),
            sspec,
            const2((SUBLANES, LANES)),
            const2((2, SUBLANES, LANES)),
            const2((1, DN_HEAD_DIM)),
        ],
        out_specs=(rows(DN_W), sspec),
        out_shape=(jax.ShapeDtypeStruct((nb * nchunk * c, DN_W), F32),
                   jax.ShapeDtypeStruct((nb, DN_HEADS, DN_HEAD_DIM, DN_HEAD_DIM), F32)),
        scratch_shapes=[pltpu.VMEM((DN_HEADS, DN_HEAD_DIM, DN_HEAD_DIM), F32),
                        pltpu.VMEM((xp_rows, CONV_CH), F32)],
        compiler_params=_cparams(("parallel", "arbitrary")),
        name="deltanet_c%d" % c,
    )(dqkv, tail, w_conv, ba, bat, dz, s0, par_c, par_r, norm_g)


def _mm_kernel(x_ref, w_ref, o_ref):
    o_ref[...] = _dot(x_ref[...].astype(BF16), w_ref[...])


def _mm(x, w_b, tm):
    t, kdim = x.shape
    n = w_b.shape[1]
    return pl.pallas_call(
        _mm_kernel,
        grid=(t // tm,),
        in_specs=[pl.BlockSpec((tm, kdim), lambda i: (i, 0)),
                  pl.BlockSpec((kdim, n), lambda i: (0, 0))],
        out_specs=pl.BlockSpec((tm, n), lambda i: (i, 0)),
        out_shape=jax.ShapeDtypeStruct((t, n), F32),
        compiler_params=_cparams(("parallel",)),
        name="mm",
    )(x, w_b)


def _mm_ln_kernel(*refs, n_lhs):
    lhs = refs[:n_lhs]
    ws = refs[n_lhs:2 * n_lhs]
    res_ref, g_ref, b_ref, o_ref = refs[2 * n_lhs:]
    acc = _dot(lhs[0][...].astype(BF16), ws[0][...])
    for a_ref, w_ref in zip(lhs[1:], ws[1:]):
        acc = acc + _dot(a_ref[...].astype(BF16), w_ref[...])
    o_ref[...] = _layer_norm(DEEPNORM_ALPHA * res_ref[...] + acc, g_ref[...], b_ref[...])


def _mm_ln(lhs, ws, resid, g, b, tm):
    t = resid.shape[0]
    n_lhs = len(lhs)
    in_specs = [pl.BlockSpec((tm, a.shape[1]), lambda i: (i, 0)) for a in lhs]
    in_specs += [pl.BlockSpec(w.shape, lambda i: (0, 0)) for w in ws]
    in_specs += [pl.BlockSpec((tm, D_MODEL), lambda i: (i, 0)),
                 pl.BlockSpec((1, D_MODEL), lambda i: (0, 0)),
                 pl.BlockSpec((1, D_MODEL), lambda i: (0, 0))]
    return pl.pallas_call(
        functools.partial(_mm_ln_kernel, n_lhs=n_lhs),
        grid=(t // tm,),
        in_specs=in_specs,
        out_specs=pl.BlockSpec((tm, D_MODEL), lambda i: (i, 0)),
        out_shape=jax.ShapeDtypeStruct((t, D_MODEL), F32),
        compiler_params=_cparams(("parallel",)),
        name="mm_ln%d" % n_lhs,
    )(*lhs, *ws, resid, g, b)


def _xattn_kernel(q_ref, k_ref, v_ref, o_ref):
    q = q_ref[...].astype(BF16)
    k = k_ref[...].astype(BF16)
    v = v_ref[...].astype(BF16)
    outs = []
    for h in range(X_HEADS):
        lo = h * X_HEAD_DIM
        s = _dot_nt(q[:, lo:lo + X_HEAD_DIM], k[:, lo:lo + X_HEAD_DIM]) * (X_HEAD_DIM ** -0.5)
        m = jnp.max(s, -1, keepdims=True)
        p = jnp.exp(s - m)
        p = p * (1.0 / jnp.sum(p, -1, keepdims=True))
        outs.append(_dot(p.astype(BF16), v[:, lo:lo + X_HEAD_DIM]))
    o_ref[...] = jnp.concatenate(outs, -1)


def _xattn(q, mk, mv, nb, nblk, tq):
    qspec = pl.BlockSpec((tq, D_MODEL), lambda b, i: (b * nblk + i, 0))
    mspec = pl.BlockSpec((MEM_LEN, D_MODEL), lambda b, i: (b, 0))
    return pl.pallas_call(
        _xattn_kernel,
        grid=(nb, nblk),
        in_specs=[qspec, mspec, mspec],
        out_specs=qspec,
        out_shape=jax.ShapeDtypeStruct(q.shape, F32),
        compiler_params=_cparams(("parallel", "arbitrary")),
        name="xattn",
    )(q, mk, mv)


class _TopK:
    def __init__(self, s, key=None):
        self.s = s
        self.key = lax.broadcasted_iota(jnp.int32, s.shape, 0).astype(F32) if key is None else key
        self.vals, self.ids = [], []

    def step(self):
        m = jnp.max(self.s, axis=0, keepdims=True)
        idx = jnp.min(jnp.where(self.s == m, self.key, jnp.inf), axis=0, keepdims=True)
        self.vals.append(m)
        self.ids.append(idx)
        self.s = jnp.where(self.key == idx, -jnp.inf, self.s)

    def result(self):
        return jnp.concatenate(self.vals, 0), jnp.concatenate(self.ids, 0)


def _pair_candidates(v1, v2):
    sub = lax.broadcasted_iota(jnp.int32, (SUBLANES, v1.shape[1]), 0)
    subf = sub.astype(F32)
    vals, keys = [], []
    for a in range(PEER_TOPK // 2):
        nb_a = PEER_TOPK // (a + 1)
        for b0 in range(0, nb_a, SUBLANES):
            part = v1[a:a + 1] + v2[b0:b0 + SUBLANES]
            if nb_a - b0 < SUBLANES:
                part = jnp.where(sub < nb_a - b0, part, -jnp.inf)
            vals.append(part)
            keys.append(subf + float(a * PEER_TOPK + b0))
    vals.append(v1[PEER_TOPK // 2:] + v2[0:1])
    keys.append((subf + float(PEER_TOPK // 2)) * float(PEER_TOPK))
    return jnp.concatenate(vals, 0), jnp.concatenate(keys, 0)


def _peer_sel_kernel(x_ref, wq_ref, k1_ref, k2_ref, i1_ref, i2_ref, g_ref):
    half = PEER_QDIM // 2
    q = _dot(x_ref[...].astype(BF16), wq_ref[...])
    k1 = k1_ref[...]
    k2 = k2_ref[...]

    def key_scores(h):
        lo = h * PEER_QDIM
        s1 = _dot_nt(k1, q[:, lo:lo + half].astype(BF16))
        s2 = _dot_nt(k2, q[:, lo + half:lo + PEER_QDIM].astype(BF16))
        return _TopK(s1), _TopK(s2)

    def finish(h, t1, t2, tc):
        _, i1 = t1.result()
        _, i2 = t2.result()
        sc, ci = tc.result()
        ca = jnp.floor(ci * (1.0 / PEER_TOPK))
        cb = ci - ca * PEER_TOPK
        e1 = jnp.zeros_like(ci)
        e2 = jnp.zeros_like(ci)
        for a in range(PEER_TOPK):
            e1 = jnp.where(ca == float(a), i1[a:a + 1], e1)
            e2 = jnp.where(cb == float(a), i2[a:a + 1], e2)
        ex = jnp.exp(sc - sc[0:1])
        gate = ex * (1.0 / jnp.sum(ex, axis=0, keepdims=True))
        rows = slice(h * PEER_TOPK, (h + 1) * PEER_TOPK)
        i1_ref[0, rows, :] = e1
        i2_ref[0, rows, :] = e2
        g_ref[0, rows, :] = gate

    prev = None
    for h in range(PEER_HEADS + 1):
        cur = key_scores(h) if h < PEER_HEADS else None
        for _ in range(PEER_TOPK):
            if cur is not None:
                cur[0].step()
                cur[1].step()
            if prev is not None:
                prev[2].step()
        if prev is not None:
            finish(h - 1, *prev)
        if cur is not None:
            v1, _ = cur[0].result()
            v2, _ = cur[1].result()
            prev = (cur[0], cur[1], _TopK(*_pair_candidates(v1, v2)))


def _peer_sel(x, wq_b, k1_b, k2_b):
    t = x.shape[0]
    tb = LANES
    nblk = t // tb
    nsel = PEER_HEADS * PEER_TOPK
    ospec = pl.BlockSpec((1, nsel, tb), lambda i: (i, 0, 0))
    oshape = jax.ShapeDtypeStruct((nblk, nsel, tb), F32)
    return pl.pallas_call(
        _peer_sel_kernel,
        grid=(nblk,),
        in_specs=[pl.BlockSpec((tb, D_MODEL), lambda i: (i, 0)),
                  pl.BlockSpec((D_MODEL, PEER_HEADS * PEER_QDIM), lambda i: (0, 0)),
                  pl.BlockSpec((N_KEYS, PEER_QDIM // 2), lambda i: (0, 0)),
                  pl.BlockSpec((N_KEYS, PEER_QDIM // 2), lambda i: (0, 0))],
        out_specs=(ospec, ospec, ospec),
        out_shape=(oshape, oshape, oshape),
        compiler_params=_cparams(("parallel",)),
        name="peer_sel",
    )(x, wq_b, k1_b, k2_b)


G_PITCH_PAD = 8
G_TOKENS = 8


def _peer_g_kernel(i1_ref, i2_ref, g_ref, o_ref, i1_scr, i2_scr, g_scr, buf, *, tb):
    pitch = tb + G_PITCH_PAD
    nsel = PEER_HEADS * PEER_TOPK
    i1_scr[...] = i1_ref[0].T
    i2_scr[...] = i2_ref[0].T
    g_scr[...] = g_ref[0].T
    kidx = lax.broadcasted_iota(jnp.int32, (N_KEYS, nsel), 0).astype(F32)

    def body(tg, carry):
        t0 = pl.multiple_of(tg * G_TOKENS, G_TOKENS)
        r1 = i1_scr[pl.ds(t0, G_TOKENS), :]
        r2 = i2_scr[pl.ds(t0, G_TOKENS), :]
        rg = g_scr[pl.ds(t0, G_TOKENS), :]
        ats = [jnp.where(kidx == r1[j:j + 1], rg[j:j + 1], 0.0).astype(BF16) for j in range(G_TOKENS)]
        bts = [jnp.where(kidx == r2[j:j + 1], 1.0, 0.0).astype(BF16) for j in range(G_TOKENS)]
        gts = [_dot_nt(ats[j], bts[j]) for j in range(G_TOKENS)]
        for j in range(G_TOKENS):
            for r in range(N_KEYS // SUBLANES):
                buf[pl.ds(r * SUBLANES * pitch + t0 + j, SUBLANES, stride=pitch), :] = (
                    gts[j][r * SUBLANES:(r + 1) * SUBLANES, :])
        return carry

    lax.fori_loop(0, tb // G_TOKENS, body, 0)
    for i1 in range(N_KEYS):
        o_ref[i1] = buf[i1 * pitch:i1 * pitch + tb, :].astype(BF16)


def _peer_g(i1, i2, g):
    nblk, nsel, tb = i1.shape
    ispec = pl.BlockSpec((1, nsel, tb), lambda i: (i, 0, 0))
    return pl.pallas_call(
        functools.partial(_peer_g_kernel, tb=tb),
        grid=(nblk,),
        in_specs=[ispec, ispec, ispec],
        out_specs=pl.BlockSpec((N_KEYS, tb, N_KEYS), lambda i: (0, i, 0)),
        out_shape=jax.ShapeDtypeStruct((N_KEYS, nblk * tb, N_KEYS), BF16),
        scratch_shapes=[pltpu.VMEM((tb, nsel), F32), pltpu.VMEM((tb, nsel), F32),
                        pltpu.VMEM((tb, nsel), F32),
                        pltpu.VMEM((N_KEYS * (tb + G_PITCH_PAD), N_KEYS), F32)],
        compiler_params=_cparams(("parallel",)),
        name="peer_g",
    )(i1, i2, g)


PEER_I1_PER_CHUNK = 8
PEER_CHUNK = PEER_I1_PER_CHUNK * N_KEYS


def _gelu_tanh(x):
    return 0.5 * x * (1.0 + jnp.tanh(math.sqrt(2.0 / math.pi) * (x + 0.044715 * (x * x * x))))


def _peer_ffn_kernel(x_ref, u_ref, v_ref, gm_ref, lg_ref, lb_ref, o_ref, xb_scr, acc_scr):
    c = pl.program_id(1)

    @pl.when(c == 0)
    def _():
        xb_scr[...] = x_ref[...].astype(BF16)
        acc_scr[...] = jnp.zeros_like(acc_scr)

    hid = _dot_nt(xb_scr[...], u_ref[...])
    ws = []
    for j in range(PEER_I1_PER_CHUNK):
        hj = hid[:, j * N_KEYS:(j + 1) * N_KEYS]
        ws.append((_gelu_tanh(hj) * gm_ref[j].astype(F32)).astype(BF16))
    acc_scr[...] += _dot(jnp.concatenate(ws, -1), v_ref[...])

    @pl.when(c == pl.num_programs(1) - 1)
    def _():
        o_ref[...] = _layer_norm(DEEPNORM_ALPHA * x_ref[...] + acc_scr[...], lg_ref[...], lb_ref[...])


def _peer_ffn(x, u_b, v_b, gmap, lg, lb, tm):
    t = x.shape[0]
    nchunk = u_b.shape[0] // PEER_CHUNK
    return pl.pallas_call(
        _peer_ffn_kernel,
        grid=(t // tm, nchunk),
        in_specs=[pl.BlockSpec((tm, D_MODEL), lambda i, c: (i, 0)),
                  pl.BlockSpec((PEER_CHUNK, D_MODEL), lambda i, c: (c, 0)),
                  pl.BlockSpec((PEER_CHUNK, D_MODEL), lambda i, c: (c, 0)),
                  pl.BlockSpec((PEER_I1_PER_CHUNK, tm, N_KEYS), lambda i, c: (c, i, 0)),
                  pl.BlockSpec((1, D_MODEL), lambda i, c: (0, 0)),
                  pl.BlockSpec((1, D_MODEL), lambda i, c: (0, 0))],
        out_specs=pl.BlockSpec((tm, D_MODEL), lambda i, c: (i, 0)),
        out_shape=jax.ShapeDtypeStruct((t, D_MODEL), F32),
        scratch_shapes=[pltpu.VMEM((tm, D_MODEL), BF16), pltpu.VMEM((tm, D_MODEL), F32)],
        compiler_params=_cparams(("parallel", "arbitrary")),
        name="peer_ffn",
    )(x, u_b, v_b, gmap, lg, lb)


def _peer(x2, wts, tm):
    i1, i2, g = _peer_sel(x2, wts["w_pq"], wts["keys1"], wts["keys2"])
    gmap = _peer_g(i1, i2, g)
    return _peer_ffn(x2, wts["peer_u"], wts["peer_v"], gmap, wts["ln3_g"], wts["ln3_b"], tm)


def _rope_tables(pos):
    half = ATT_HEAD_DIM // 2
    inv_freq = ROPE_THETA ** (-jnp.arange(half, dtype=F32) / half)
    ang = pos.astype(F32)[:, None] * inv_freq[None, :]
    cos = jnp.cos(ang)
    sin = jnp.sin(ang)
    reps = LANES // ATT_HEAD_DIM
    cos_t = jnp.tile(jnp.concatenate([cos, cos], -1), (1, reps))
    sin_t = jnp.tile(jnp.concatenate([-sin, sin], -1), (1, reps))
    return cos_t, sin_t


def _prep_weights(w_in, w_conv, dn_a_log, dn_dt_bias, dn_norm_g, attn_sinks, w_o, ln1_g, ln1_b,
                  w_cq, w_ck, w_cv, w_co, ln2_g, ln2_b, w_pq, peer_keys1, peer_keys2,
                  peer_u, peer_v, ln3_g, ln3_b):
    o_dqkv = ATT_Q + 2 * ATT_KV
    o_ba = o_dqkv + CONV_CH
    o_dz = o_ba + 2 * DN_HEADS
    w_ba = w_in[:, o_ba:o_dz]
    w_p = jnp.concatenate([w_in[:, :o_ba], w_in[:, o_dz:], w_ba,
                           jnp.zeros((D_MODEL, LANES - 2 * DN_HEADS), F32)], -1).astype(BF16)
    lane_par = lambda p: jnp.zeros((LANES,), F32).at[DN_HEADS:2 * DN_HEADS].set(p)
    par_c = jnp.zeros((SUBLANES, LANES), F32).at[0].set(lane_par(dn_a_log)).at[1].set(lane_par(dn_dt_bias))
    row_par = lambda p: jnp.broadcast_to(
        jnp.zeros((SUBLANES,), F32).at[DN_HEADS:2 * DN_HEADS].set(p)[:, None], (SUBLANES, LANES))
    par_r = jnp.stack([row_par(dn_a_log), row_par(dn_dt_bias)])
    row = lambda p: p.reshape(1, -1)
    return dict(
        w_in=w_p, w_ba_t=w_ba.T.astype(BF16), w_conv=w_conv, par_c=par_c, par_r=par_r,
        norm_g=row(dn_norm_g), sinks=attn_sinks,
        w_o_a=w_o[:ATT_Q].astype(BF16), w_o_d=w_o[ATT_Q:].astype(BF16),
        ln1_g=row(ln1_g), ln1_b=row(ln1_b),
        w_cq=w_cq.astype(BF16), w_ck=w_ck.astype(BF16), w_cv=w_cv.astype(BF16), w_co=w_co.astype(BF16),
        ln2_g=row(ln2_g), ln2_b=row(ln2_b),
        w_pq=w_pq.reshape(D_MODEL, PEER_HEADS * PEER_QDIM).astype(BF16),
        keys1=peer_keys1.astype(BF16), keys2=peer_keys2.astype(BF16),
        peer_u=peer_u.astype(BF16), peer_v=peer_v.astype(BF16),
        ln3_g=row(ln3_g), ln3_b=row(ln3_b),
    )


def _tile(t, pref):
    while t % pref:
        pref //= 2
    return pref


def _layer_tail(xf, a_out, dn_out, mk, mv, nb, seq, wts):
    t = xf.shape[0]
    tm = _tile(t, 256)
    x1 = _mm_ln([a_out, dn_out], [wts["w_o_a"], wts["w_o_d"]], xf, wts["ln1_g"], wts["ln1_b"], tm)
    qc = _mm(x1, wts["w_cq"], tm)
    tq = _tile(seq, 512)
    xo = _xattn(qc, mk, mv, nb, seq // tq, tq)
    x2 = _mm_ln([xo], [wts["w_co"]], x1, wts["ln2_g"], wts["ln2_b"], tm)
    return _peer(x2, wts, _tile(t, 512))


def _prompt_layer(x, mem, wts):
    nb, seq, _ = x.shape
    t = nb * seq
    xf = x.reshape(t, D_MODEL)
    cos, sin = _rope_tables(jnp.arange(seq, dtype=jnp.int32))
    q, k, v, dqkv, dz, ba, bat = _in_proj(xf, wts["w_in"], wts["w_ba_t"], cos, sin, _tile(seq, 256))
    nblk = seq // WINDOW
    a_out = _swa(wts["sinks"], q, k, v, nb, nblk, WINDOW, WINDOW, True)
    c = DN_CHUNK if seq % DN_CHUNK == 0 else seq
    nchunk = seq // c
    bat_c = bat.reshape(SUBLANES, nb * nchunk, c).transpose(1, 0, 2)
    tail = jnp.zeros((nb * SUBLANES, CONV_CH), F32)
    s0 = jnp.zeros((nb, DN_HEADS, DN_HEAD_DIM, DN_HEAD_DIM), F32)
    dn_out, s_new = _deltanet(dqkv, tail, wts["w_conv"], ba, bat_c, dz, s0, wts["par_c"], wts["par_r"],
                              wts["norm_g"], nb, nchunk, c, c)
    memf = mem.reshape(nb * MEM_LEN, D_MODEL)
    mk = _mm(memf, wts["w_ck"], _tile(nb * MEM_LEN, 256))
    mv = _mm(memf, wts["w_cv"], _tile(nb * MEM_LEN, 256))
    y = _layer_tail(xf, a_out, dn_out, mk, mv, nb, seq, wts)
    new_k = k.reshape(nb, seq, ATT_KV_HEADS, ATT_HEAD_DIM)[:, -WINDOW:]
    new_v = v.reshape(nb, seq, ATT_KV_HEADS, ATT_HEAD_DIM)[:, -WINDOW:]
    new_conv = dqkv.reshape(nb, seq, CONV_CH)[:, -(CONV_WIDTH - 1):]
    mk4 = mk.reshape(nb, MEM_LEN, X_HEADS, X_HEAD_DIM)
    mv4 = mv.reshape(nb, MEM_LEN, X_HEADS, X_HEAD_DIM)
    return y.reshape(nb, seq, D_MODEL), new_k, new_v, s_new, new_conv, mk4, mv4


def _sample_layer(x, past_k, past_v, s0, conv_buf, mem_k, mem_v, wts):
    nb, seq, _ = x.shape
    assert CONV_WIDTH - 1 <= seq <= SAMPLE_PAD
    pad = SAMPLE_PAD
    t = nb * pad
    xf = jnp.pad(x, ((0, 0), (0, pad - seq), (0, 0))).reshape(t, D_MODEL)
    cos8, sin8 = _rope_tables(PAST_LEN + jnp.arange(pad, dtype=jnp.int32))
    tm = _tile(t, 256)
    cos = jnp.tile(cos8, (tm // pad, 1))
    sin = jnp.tile(sin8, (tm // pad, 1))
    q, k, v, dqkv, dz, ba, bat = _in_proj(xf, wts["w_in"], wts["w_ba_t"], cos, sin, tm)
    npast = past_k.shape[1]
    kk = jnp.concatenate([past_k.reshape(nb, npast, ATT_KV), k.reshape(nb, pad, ATT_KV)], 1)
    vv = jnp.concatenate([past_v.reshape(nb, npast, ATT_KV), v.reshape(nb, pad, ATT_KV)], 1)
    nk = npast + pad
    a_out = _swa(wts["sinks"], q, kk.reshape(nb * nk, ATT_KV), vv.reshape(nb * nk, ATT_KV),
                 nb, 1, pad, nk, False)
    bat_c = bat.reshape(SUBLANES, nb, pad).transpose(1, 0, 2)
    tail = jnp.pad(conv_buf, ((0, 0), (SUBLANES - (CONV_WIDTH - 1), 0), (0, 0))).reshape(nb * SUBLANES, CONV_CH)
    dn_out, s_new = _deltanet(dqkv, tail, wts["w_conv"], ba, bat_c, dz, s0, wts["par_c"], wts["par_r"],
                              wts["norm_g"], nb, 1, pad, seq)
    mk = mem_k.reshape(nb * MEM_LEN, D_MODEL)
    mv = mem_v.reshape(nb * MEM_LEN, D_MODEL)
    y = _layer_tail(xf, a_out, dn_out, mk, mv, nb, pad, wts)
    new_k = kk[:, seq:seq + npast].reshape(nb, npast, ATT_KV_HEADS, ATT_HEAD_DIM)
    new_v = vv[:, seq:seq + npast].reshape(nb, npast, ATT_KV_HEADS, ATT_HEAD_DIM)
    new_conv = dqkv.reshape(nb, pad, CONV_CH)[:, seq - (CONV_WIDTH - 1):seq]
    return y.reshape(nb, pad, D_MODEL)[:, :seq], new_k, new_v, s_new, new_conv


def kernel(x_prompt, x_sample, cache_swa_k, cache_swa_v, state_dn, state_dn_conv, cache_mem_k, cache_mem_v, mem_prompt, w_in, w_conv, dn_a_log, dn_dt_bias, dn_norm_g, attn_sinks, w_o, ln1_g, ln1_b, w_cq, w_ck, w_cv, w_co, ln2_g, ln2_b, w_pq, peer_keys1, peer_keys2, peer_u, peer_v, ln3_g, ln3_b):
    assert w_in.shape[0] == DEPTH == 1
    wts = _prep_weights(w_in[0], w_conv[0], dn_a_log[0], dn_dt_bias[0], dn_norm_g[0], attn_sinks[0],
                        w_o[0], ln1_g[0], ln1_b[0], w_cq[0], w_ck[0], w_cv[0], w_co[0], ln2_g[0], ln2_b[0],
                        w_pq[0], peer_keys1[0], peer_keys2[0], peer_u[0], peer_v[0], ln3_g[0], ln3_b[0])
    y_p, pk, pv, ps, pc, pmk, pmv = _prompt_layer(x_prompt, mem_prompt, wts)
    y_s, sk, sv, ss, sc = _sample_layer(x_sample, cache_swa_k[0], cache_swa_v[0], state_dn[0],
                                        state_dn_conv[0], cache_mem_k[0], cache_mem_v[0], wts)
    st = lambda a: a[None]
    return (y_p, y_s, st(pk), st(pv), st(ps), st(pc), st(pmk), st(pmv), st(sk), st(sv), st(ss), st(sc))
```

```python
import functools
import math

import jax
import jax.numpy as jnp
from jax import lax
from jax.experimental import pallas as pl
from jax.experimental.pallas import tpu as pltpu

F32 = jnp.float32
BF16 = jnp.bfloat16

D_MODEL = 1024
DEPTH = 1
PAST_LEN = 16384
ATT_HEADS = 8
ATT_KV_HEADS = 2
ATT_HEAD_DIM = 64
ATT_GROUP = ATT_HEADS // ATT_KV_HEADS
WINDOW = 128
ROPE_THETA = 10000.0
DN_HEADS = 4
DN_HEAD_DIM = 128
CONV_WIDTH = 4
DN_CHUNK = 64
MEM_LEN = 256
X_HEADS = 4
X_HEAD_DIM = D_MODEL // X_HEADS
N_KEYS = 128
PEER_HEADS = 8
PEER_QDIM = 256
PEER_TOPK = 16
LN_EPS = 1e-5
RMS_EPS = 1e-6
NEG_INF = -1e30
DEEPNORM_ALPHA = (2 * DEPTH) ** 0.25

ATT_Q = ATT_HEADS * ATT_HEAD_DIM
ATT_KV = ATT_KV_HEADS * ATT_HEAD_DIM
DN_W = DN_HEADS * DN_HEAD_DIM
CONV_CH = 3 * DN_W
IN_PACKED = ATT_Q + 2 * ATT_KV + CONV_CH + DN_W + 128

LANES = 128
SUBLANES = 8
VMEM_LIMIT = 48 * 1024 * 1024
SAMPLE_PAD = 8


def _cparams(sem):
    return pltpu.CompilerParams(dimension_semantics=sem, vmem_limit_bytes=VMEM_LIMIT)


def _dot(a, b):
    return jnp.dot(a, b, preferred_element_type=F32)


def _dot_nt(a, b):
    return lax.dot_general(a, b, (((1,), (1,)), ((), ())), preferred_element_type=F32)


def _dot_tn(a, b):
    return lax.dot_general(a, b, (((0,), (0,)), ((), ())), preferred_element_type=F32)


def _split3(x):
    x0 = x.astype(BF16)
    r = x - x0.astype(F32)
    x1 = r.astype(BF16)
    x2 = (r - x1.astype(F32)).astype(BF16)
    return x0, x1, x2


def _dot_exact_lhs(a_bf16, b):
    b0, b1, b2 = _split3(b)
    return _dot(a_bf16, b0) + _dot(a_bf16, b1) + _dot(a_bf16, b2)


def _dot_exact_rhs(a, b_bf16):
    a0, a1, a2 = _split3(a)
    return _dot(a0, b_bf16) + _dot(a1, b_bf16) + _dot(a2, b_bf16)


def _dot_hi(a, b):
    a0 = a.astype(BF16)
    a1 = (a - a0.astype(F32)).astype(BF16)
    b0 = b.astype(BF16)
    b1 = (b - b0.astype(F32)).astype(BF16)
    return _dot(a0, b0) + _dot(a0, b1) + _dot(a1, b0)


def _sigmoid(x):
    return 1.0 / (1.0 + jnp.exp(-x))


def _layer_norm(y, g, b):
    mu = jnp.mean(y, -1, keepdims=True)
    d = y - mu
    var = jnp.mean(d * d, -1, keepdims=True)
    return d * lax.rsqrt(var + LN_EPS) * g + b


def _in_proj_kernel(x_ref, w_ref, wt_ref, cos_ref, sin_ref,
                    q_ref, k_ref, v_ref, dqkv_ref, dz_ref, ba_ref, bat_ref):
    xb = x_ref[...].astype(BF16)
    h = _dot(xb, w_ref[...])
    cos = cos_ref[...]
    sin = sin_ref[...]
    lane = lax.broadcasted_iota(jnp.int32, cos.shape, 1)
    first = (lane & (ATT_HEAD_DIM - 1)) < (ATT_HEAD_DIM // 2)

    def rope(t):
        up = pltpu.roll(t, LANES - ATT_HEAD_DIM // 2, 1)
        dn = pltpu.roll(t, ATT_HEAD_DIM // 2, 1)
        return t * cos + jnp.where(first, up, dn) * sin

    for j in range(ATT_Q // LANES):
        q_ref[:, j * LANES:(j + 1) * LANES] = rope(h[:, j * LANES:(j + 1) * LANES]).astype(q_ref.dtype)
    k_ref[...] = rope(h[:, ATT_Q:ATT_Q + ATT_KV])
    o = ATT_Q + ATT_KV
    v_ref[...] = h[:, o:o + ATT_KV]
    o += ATT_KV
    dqkv_ref[...] = h[:, o:o + CONV_CH]
    o += CONV_CH
    dz_ref[...] = h[:, o:o + DN_W]
    o += DN_W
    ba_ref[...] = h[:, o:o + LANES]
    bat_ref[...] = _dot_nt(wt_ref[...], xb)


def _in_proj(x, w_p, w_t, cos, sin, tm):
    t = x.shape[0]
    nt = t // tm
    ncos = cos.shape[0] // tm
    row = lambda width: pl.BlockSpec((tm, width), lambda i: (i, 0))
    out_shape = (
        jax.ShapeDtypeStruct((t, ATT_Q), BF16),
        jax.ShapeDtypeStruct((t, ATT_KV), F32),
        jax.ShapeDtypeStruct((t, ATT_KV), F32),
        jax.ShapeDtypeStruct((t, CONV_CH), F32),
        jax.ShapeDtypeStruct((t, DN_W), F32),
        jax.ShapeDtypeStruct((t, LANES), F32),
        jax.ShapeDtypeStruct((SUBLANES, t), F32),
    )
    return pl.pallas_call(
        _in_proj_kernel,
        grid=(nt,),
        in_specs=[
            row(D_MODEL),
            pl.BlockSpec((D_MODEL, IN_PACKED), lambda i: (0, 0)),
            pl.BlockSpec((SUBLANES, D_MODEL), lambda i: (0, 0)),
            pl.BlockSpec((tm, LANES), lambda i: (i % ncos, 0)),
            pl.BlockSpec((tm, LANES), lambda i: (i % ncos, 0)),
        ],
        out_specs=(row(ATT_Q), row(ATT_KV), row(ATT_KV), row(CONV_CH), row(DN_W), row(LANES),
                   pl.BlockSpec((SUBLANES, tm), lambda i: (0, i))),
        out_shape=out_shape,
        compiler_params=_cparams(("parallel",)),
        name="in_proj",
    )(x, w_p, w_t, cos, sin)


def _swa_kernel(sinks_ref, q_ref, *refs, has_prev, units):
    if has_prev:
        kp_ref, kc_ref, vp_ref, vc_ref, o_ref = refs
        k = jnp.concatenate([kp_ref[...], kc_ref[...]], 0)
        v = jnp.concatenate([vp_ref[...], vc_ref[...]], 0)
    else:
        kc_ref, vc_ref, o_ref = refs
        k = kc_ref[...]
        v = vc_ref[...]
    q = q_ref[...].astype(F32)
    r, nk = q.shape[0] // units, k.shape[0] // units
    gsz = ATT_GROUP if r < LANES else 1
    ngrp = ATT_HEADS // gsz
    rows = gsz * r
    qi = lax.broadcasted_iota(jnp.int32, (rows, nk), 0) & (r - 1)
    kj = lax.broadcasted_iota(jnp.int32, (rows, nk), 1)
    valid = (kj >= qi) & (kj <= qi + WINDOW)
    if has_prev:
        jmin = jnp.where(pl.program_id(1) == 0, WINDOW, 0)
        valid = valid & (kj >= jmin)
    ug = [(u, g) for u in range(units) for g in range(ngrp)]
    kvh = lambda g: (g * gsz) // ATT_GROUP
    hd = lambda h: slice(h * ATT_HEAD_DIM, (h + 1) * ATT_HEAD_DIM)
    kb = {x: k[x[0] * nk:(x[0] + 1) * nk, hd(kvh(x[1]))].astype(BF16) for x in ug}
    vb = {x: v[x[0] * nk:(x[0] + 1) * nk, hd(kvh(x[1]))].astype(BF16) for x in ug}
    if gsz == 1:
        sink = {g: sinks_ref[g] for g in range(ngrp)}
    else:
        sink = {g: jnp.concatenate([jnp.full((r, 1), sinks_ref[g * gsz + i], F32) for i in range(gsz)], 0)
                for g in range(ngrp)}
    qs = {x: jnp.concatenate([q[x[0] * r:(x[0] + 1) * r, hd(x[1] * gsz + i)] for i in range(gsz)], 0).astype(BF16)
          for x in ug}
    s = {x: _dot_nt(qs[x], kb[x]) * (ATT_HEAD_DIM ** -0.5) for x in ug}
    p = {}
    for x in ug:
        sx = jnp.where(valid, s[x], NEG_INF)
        m = jnp.maximum(jnp.max(sx, -1, keepdims=True), sink[x[1]])
        e = jnp.exp(sx - m)
        den = jnp.sum(e, -1, keepdims=True) + jnp.exp(sink[x[1]] - m)
        p[x] = (e * (1.0 / den)).astype(BF16)
    o = {x: _dot(p[x], vb[x]) for x in ug}
    outs = []
    for u in range(units):
        outs.append(jnp.concatenate([o[(u, h // gsz)][(h % gsz) * r:(h % gsz + 1) * r]
                                     for h in range(ATT_HEADS)], -1))
    o_ref[...] = jnp.concatenate(outs, 0).astype(o_ref.dtype)


def _swa(sinks, q, k, v, nb, nblk, rq, nk, has_prev, units):
    assert rq & (rq - 1) == 0 and nb % units == 0 and (units == 1 or nblk == 1)
    smem = pl.BlockSpec(memory_space=pltpu.SMEM)
    qspec = pl.BlockSpec((units * rq, ATT_Q), lambda b, n: (b * nblk + n, 0))
    cur = pl.BlockSpec((units * nk, ATT_KV), lambda b, n: (b * nblk + n, 0))
    if has_prev:
        prev = pl.BlockSpec((nk, ATT_KV), lambda b, n: (b * nblk + jnp.maximum(n - 1, 0), 0))
        in_specs = [smem, qspec, prev, cur, prev, cur]
        args = (sinks, q, k, k, v, v)
    else:
        in_specs = [smem, qspec, cur, cur]
        args = (sinks, q, k, v)
    return pl.pallas_call(
        functools.partial(_swa_kernel, has_prev=has_prev, units=units),
        grid=(nb // units, nblk),
        in_specs=in_specs,
        out_specs=qspec,
        out_shape=jax.ShapeDtypeStruct(q.shape, BF16),
        compiler_params=_cparams(("parallel", "arbitrary")),
        name="swa_prev" if has_prev else "swa_cache",
    )(*args)


def _deltanet_kernel(x_ref, tail_ref, wc_ref, ba_ref, bat_ref, dz_ref, s0_ref,
                     pc_ref, pr_ref, ng_ref, o_ref, s_ref, s_scr, xp_scr, *, c, c_real, nsq, group, chained):
    n = pl.program_id(1)
    rows = group * c
    wc = wc_ref[...]
    base = SUBLANES - (CONV_WIDTH - 1)

    def conv(lo, nrow):
        acc = xp_scr[lo + base:lo + base + nrow] * wc[0:1]
        for j in range(1, CONV_WIDTH):
            acc = acc + xp_scr[lo + base + j:lo + base + j + nrow] * wc[j:j + 1]
        return acc

    if chained:
        @pl.when(n == 0)
        def _():
            s_scr[...] = s0_ref[0]
            xp_scr[0:SUBLANES] = tail_ref[...]

        xp_scr[SUBLANES:SUBLANES + rows] = x_ref[...]
        y = conv(0, rows)
        xp_scr[0:SUBLANES] = xp_scr[rows:rows + SUBLANES]
    else:
        seg = SUBLANES + c
        ys = []
        for u in range(group):
            xp_scr[u * seg:u * seg + SUBLANES] = tail_ref[u * SUBLANES:(u + 1) * SUBLANES]
            xp_scr[u * seg + SUBLANES:(u + 1) * seg] = x_ref[u * c:(u + 1) * c]
            ys.append(conv(u * seg, c))
        y = jnp.concatenate(ys, 0)
    y = y * _sigmoid(y)

    ba = ba_ref[...]
    bat = jnp.concatenate([bat_ref[u] for u in range(group)], -1)
    beta_c = _sigmoid(ba)
    g_c = -jnp.exp(pc_ref[0:1]) * _softplus(ba + pc_ref[1:2])
    beta_r = _sigmoid(bat)
    g_r = -jnp.exp(pr_ref[0][:, 0:1]) * _softplus(bat + pr_ref[1][:, 0:1])
    if c_real < c:
        rmask = (lax.broadcasted_iota(jnp.int32, ba.shape, 0) & (c - 1)) < c_real
        beta_c = jnp.where(rmask, beta_c, 0.0)
        g_c = jnp.where(rmask, g_c, 0.0)
        lmask = (lax.broadcasted_iota(jnp.int32, bat.shape, 1) & (c - 1)) < c_real
        beta_r = jnp.where(lmask, beta_r, 0.0)
        g_r = jnp.where(lmask, g_r, 0.0)

    shift = int(math.log2(c))
    bi = lax.broadcasted_iota(jnp.int32, (rows, rows), 0)
    bj = lax.broadcasted_iota(jnp.int32, (rows, rows), 1)
    same = (bi >> shift) == (bj >> shift)
    tri = jnp.where(same & (bi >= bj), 1.0, 0.0).astype(BF16)
    tri_t = jnp.where(same & (bi <= bj), 1.0, 0.0).astype(BF16)
    gc_c = _dot_exact_lhs(tri, g_c)
    gc_r = _dot_exact_rhs(g_r, tri_t)
    ii = lax.broadcasted_iota(jnp.int32, (c, c), 0)
    jj = lax.broadcasted_iota(jnp.int32, (c, c), 1)
    eye = jnp.where(ii == jj, 1.0, 0.0)
    incl = ii >= jj
    strict = ii > jj

    ng = ng_ref[...]
    dz = dz_ref[...]
    hs = range(DN_HEADS)
    uh = [(u, h) for u in range(group) for h in hs]
    rs = lambda u: slice(u * c, (u + 1) * c)
    sl = lambda base, h: slice(base + h * DN_HEAD_DIM, base + (h + 1) * DN_HEAD_DIM)
    l2 = lambda t: t * lax.rsqrt(jnp.sum(t * t, -1, keepdims=True) + RMS_EPS)
    q = {x: l2(y[rs(x[0]), sl(0, x[1])]) * (DN_HEAD_DIM ** -0.5) for x in uh}
    k = {x: l2(y[rs(x[0]), sl(DN_W, x[1])]) for x in uh}
    v = {x: y[rs(x[0]), sl(2 * DN_W, x[1])] for x in uh}
    beta = {x: beta_c[rs(x[0]), x[1]:x[1] + 1] for x in uh}
    gcc = {x: gc_c[rs(x[0]), DN_HEADS + x[1]:DN_HEADS + x[1] + 1] for x in uh}
    gcr = {x: gc_r[DN_HEADS + x[1]:DN_HEADS + x[1] + 1, rs(x[0])] for x in uh}
    decay = {x: jnp.exp(jnp.where(incl, gcc[x] - gcr[x], NEG_INF)) for x in uh}
    kbeta = {x: k[x] * beta[x] for x in uh}
    kq = {x: k[x].astype(BF16) for x in uh}
    a = {x: jnp.where(strict, _dot_nt(kbeta[x].astype(BF16), kq[x]) * decay[x], 0.0) for x in uh}
    qk = {x: (_dot_nt(q[x].astype(BF16), kq[x]) * decay[x]).astype(BF16) for x in uh}
    p = {x: eye - a[x] for x in uh}
    m = a
    for _ in range(nsq):
        m = {x: _dot_hi(m[x], m[x]) for x in uh}
        p = {x: p[x] + _dot_hi(p[x], m[x]) for x in uh}
    sol = {x: _dot_hi(p[x], jnp.concatenate([v[x] * beta[x], kbeta[x] * jnp.exp(gcc[x])], -1)) for x in uh}
    g_last = {x: gcr[x][:, c - 1:c] for x in uh}
    qg = {x: (q[x] * jnp.exp(gcc[x])).astype(BF16) for x in uh}
    kd = {x: (k[x] * jnp.exp(g_last[x] - gcc[x])).astype(BF16) for x in uh}

    def advance(xs, s_old):
        s_b = {x: s_old[x].astype(BF16) for x in xs}
        v_nb = {x: (sol[x][:, :DN_HEAD_DIM] - _dot(sol[x][:, DN_HEAD_DIM:].astype(BF16), s_b[x])).astype(BF16)
                for x in xs}
        s_new = {x: s_old[x] * jnp.exp(g_last[x]) + _dot_tn(kd[x], v_nb[x]) for x in xs}
        o = {x: _dot(qg[x], s_b[x]) + _dot(qk[x], v_nb[x]) for x in xs}
        return s_new, o

    o_all = {}
    if chained:
        s_cur = {h: s_scr[h] for h in hs}
        for u in range(group):
            xs = [(u, h) for h in hs]
            s_new, o = advance(xs, {x: s_cur[x[1]] for x in xs})
            s_cur = {h: s_new[(u, h)] for h in hs}
            o_all.update(o)
        for h in hs:
            s_scr[h] = s_cur[h]
    else:
        s_new, o_all = advance(uh, {x: s0_ref[x[0], x[1]] for x in uh})
        for x in uh:
            s_ref[x[0], x[1]] = s_new[x]

    outs = []
    for u in range(group):
        parts = []
        for h in hs:
            o = o_all[(u, h)]
            o = o * lax.rsqrt(jnp.mean(o * o, -1, keepdims=True) + RMS_EPS) * ng
            z = dz[rs(u), sl(0, h)]
            parts.append(o * (z * _sigmoid(z)))
        outs.append(jnp.concatenate(parts, -1))
    o_ref[...] = jnp.concatenate(outs, 0).astype(o_ref.dtype)

    if chained:
        @pl.when(n == pl.num_programs(1) - 1)
        def _():
            s_ref[0] = s_scr[...]


def _softplus(x):
    return jnp.maximum(x, 0.0) + jnp.log(1.0 + jnp.exp(-jnp.abs(x)))


DN_GROUP_CHUNKS = 4
DN_GROUP_SEQS = 8


def _deltanet(dqkv, tail, w_conv, ba, bat, dz, s0, par_c, par_r, norm_g, nb, nchunk, c, c_real):
    assert c & (c - 1) == 0
    nsq = max(int(math.ceil(math.log2(c))) - 1, 0)
    chained = nchunk > 1
    group = _tile(nchunk, DN_GROUP_CHUNKS) if chained else _tile(nb, DN_GROUP_SEQS)
    nstep = nchunk // group if chained else 1
    ngrid0 = nb if chained else nb // group
    nstate = 1 if chained else group
    ntail = SUBLANES if chained else group * SUBLANES
    rows = lambda width: pl.BlockSpec((group * c, width), lambda b, n: (b * nstep + n, 0))
    const2 = lambda shape: pl.BlockSpec(shape, lambda b, n: (0,) * len(shape))
    sspec = pl.BlockSpec((nstate, DN_HEADS, DN_HEAD_DIM, DN_HEAD_DIM), lambda b, n: (b, 0, 0, 0))
    xp_rows = SUBLANES + group * c if chained else group * (SUBLANES + c)
    return pl.pallas_call(
        functools.partial(_deltanet_kernel, c=c, c_real=c_real, nsq=nsq, group=group, chained=chained),
        grid=(ngrid0, nstep),
        in_specs=[
            rows(CONV_CH),
            pl.BlockSpec((ntail, CONV_CH), lambda b, n: (b, 0)),
            const2((CONV_WIDTH, CONV_CH)),
            rows(LANES),
            pl.BlockSpec((group, SUBLANES, c), lambda b, n: (b * nstep + n, 0, 0)),
            rows(DN_W),
            sspec,
            const2((SUBLANES, LANES)),
            const2((2, SUBLANES, LANES)),
            const2((1, DN_HEAD_DIM)),
        ],
        out_specs=(rows(DN_W), sspec),
        out_shape=(jax.ShapeDtypeStruct((nb * nchunk * c, DN_W), BF16),
                   jax.ShapeDtypeStruct((nb, DN_HEADS, DN_HEAD_DIM, DN_HEAD_DIM), F32)),
        scratch_shapes=[pltpu.VMEM((DN_HEADS, DN_HEAD_DIM, DN_HEAD_DIM), F32),
                        pltpu.VMEM((xp_rows, CONV_CH), F32)],
        compiler_params=_cparams(("parallel", "arbitrary")),
        name="deltanet_c%d" % c,
    )(dqkv, tail, w_conv, ba, bat, dz, s0, par_c, par_r, norm_g)


def _mm_kernel(x_ref, w_ref, o_ref):
    o_ref[...] = _dot(x_ref[...].astype(BF16), w_ref[...]).astype(o_ref.dtype)


def _mm(x, w_b, tm, out_dtype=F32):
    t, kdim = x.shape
    n = w_b.shape[1]
    return pl.pallas_call(
        _mm_kernel,
        grid=(t // tm,),
        in_specs=[pl.BlockSpec((tm, kdim), lambda i: (i, 0)),
                  pl.BlockSpec((kdim, n), lambda i: (0, 0))],
        out_specs=pl.BlockSpec((tm, n), lambda i: (i, 0)),
        out_shape=jax.ShapeDtypeStruct((t, n), out_dtype),
        compiler_params=_cparams(("parallel",)),
        name="mm",
    )(x, w_b)


def _mm_ln_kernel(*refs, n_lhs, has_next):
    lhs = refs[:n_lhs]
    ws = refs[n_lhs:2 * n_lhs]
    rest = refs[2 * n_lhs:]
    res_ref, g_ref, b_ref = rest[:3]
    acc = _dot(lhs[0][...].astype(BF16), ws[0][...])
    for a_ref, w_ref in zip(lhs[1:], ws[1:]):
        acc = acc + _dot(a_ref[...].astype(BF16), w_ref[...])
    y = _layer_norm(DEEPNORM_ALPHA * res_ref[...] + acc, g_ref[...], b_ref[...])
    if has_next:
        wn_ref, o_ref, on_ref = rest[3:]
        on_ref[...] = _dot(y.astype(BF16), wn_ref[...]).astype(on_ref.dtype)
    else:
        (o_ref,) = rest[3:]
    o_ref[...] = y


def _mm_ln(lhs, ws, resid, g, b, tm, w_next=None):
    t = resid.shape[0]
    n_lhs = len(lhs)
    row = pl.BlockSpec((tm, D_MODEL), lambda i: (i, 0))
    in_specs = [pl.BlockSpec((tm, a.shape[1]), lambda i: (i, 0)) for a in lhs]
    in_specs += [pl.BlockSpec(w.shape, lambda i: (0, 0)) for w in ws]
    in_specs += [row, pl.BlockSpec((1, D_MODEL), lambda i: (0, 0)), pl.BlockSpec((1, D_MODEL), lambda i: (0, 0))]
    args = [*lhs, *ws, resid, g, b]
    out_specs, out_shape = row, jax.ShapeDtypeStruct((t, D_MODEL), F32)
    if w_next is not None:
        in_specs.append(pl.BlockSpec(w_next.shape, lambda i: (0, 0)))
        args.append(w_next)
        out_specs = (row, pl.BlockSpec((tm, w_next.shape[1]), lambda i: (i, 0)))
        out_shape = (out_shape, jax.ShapeDtypeStruct((t, w_next.shape[1]), BF16))
    return pl.pallas_call(
        functools.partial(_mm_ln_kernel, n_lhs=n_lhs, has_next=w_next is not None),
        grid=(t // tm,),
        in_specs=in_specs,
        out_specs=out_specs,
        out_shape=out_shape,
        compiler_params=_cparams(("parallel",)),
        name="mm_ln%d" % n_lhs,
    )(*args)


def _xattn_kernel(q_ref, k_ref, v_ref, o_ref, *, units):
    q = q_ref[...].astype(F32)
    tq = q.shape[0] // units
    uh = [(u, h) for u in range(units) for h in range(X_HEADS)]
    hd = lambda h: slice(h * X_HEAD_DIM, (h + 1) * X_HEAD_DIM)
    mem = lambda ref, x: ref[x[0] * MEM_LEN:(x[0] + 1) * MEM_LEN, hd(x[1])].astype(BF16)
    qh = {x: q[x[0] * tq:(x[0] + 1) * tq, hd(x[1])].astype(BF16) for x in uh}
    s = {x: _dot_nt(qh[x], mem(k_ref, x)) * (X_HEAD_DIM ** -0.5) for x in uh}
    p = {}
    for x in uh:
        e = jnp.exp(s[x] - jnp.max(s[x], -1, keepdims=True))
        p[x] = (e * (1.0 / jnp.sum(e, -1, keepdims=True))).astype(BF16)
    o = {x: _dot(p[x], mem(v_ref, x)) for x in uh}
    outs = [jnp.concatenate([o[(u, h)] for h in range(X_HEADS)], -1) for u in range(units)]
    o_ref[...] = jnp.concatenate(outs, 0).astype(o_ref.dtype)


def _xattn(q, mk, mv, nb, nblk, tq, units):
    assert nb % units == 0 and (units == 1 or nblk == 1)
    qspec = pl.BlockSpec((units * tq, D_MODEL), lambda b, i: (b * nblk + i, 0))
    mspec = pl.BlockSpec((units * MEM_LEN, D_MODEL), lambda b, i: (b, 0))
    return pl.pallas_call(
        functools.partial(_xattn_kernel, units=units),
        grid=(nb // units, nblk),
        in_specs=[qspec, mspec, mspec],
        out_specs=qspec,
        out_shape=jax.ShapeDtypeStruct(q.shape, BF16),
        compiler_params=_cparams(("parallel", "arbitrary")),
        name="xattn",
    )(q, mk, mv)


class _TopK:
    def __init__(self, s, key=None):
        self.s = s
        self.key = lax.broadcasted_iota(jnp.int32, s.shape, 0).astype(F32) if key is None else key
        self.vals, self.ids = [], []

    def step(self):
        m = jnp.max(self.s, axis=0, keepdims=True)
        idx = jnp.min(jnp.where(self.s == m, self.key, jnp.inf), axis=0, keepdims=True)
        self.vals.append(m)
        self.ids.append(idx)
        self.s = jnp.where(self.key == idx, -jnp.inf, self.s)

    def result(self):
        return jnp.concatenate(self.vals, 0), jnp.concatenate(self.ids, 0)


def _pair_candidates(v1, v2):
    sub = lax.broadcasted_iota(jnp.int32, (SUBLANES, v1.shape[1]), 0)
    subf = sub.astype(F32)
    vals, keys = [], []
    for a in range(PEER_TOPK // 2):
        nb_a = PEER_TOPK // (a + 1)
        for b0 in range(0, nb_a, SUBLANES):
            part = v1[a:a + 1] + v2[b0:b0 + SUBLANES]
            if nb_a - b0 < SUBLANES:
                part = jnp.where(sub < nb_a - b0, part, -jnp.inf)
            vals.append(part)
            keys.append(subf + float(a * PEER_TOPK + b0))
    vals.append(v1[PEER_TOPK // 2:] + v2[0:1])
    keys.append((subf + float(PEER_TOPK // 2)) * float(PEER_TOPK))
    return jnp.concatenate(vals, 0), jnp.concatenate(keys, 0)


def _peer_sel_kernel(x_ref, wq_ref, k1_ref, k2_ref, i1_ref, i2_ref, g_ref):
    half = PEER_QDIM // 2
    q = _dot(x_ref[...].astype(BF16), wq_ref[...])
    k1 = k1_ref[...]
    k2 = k2_ref[...]

    def key_scores(h):
        lo = h * PEER_QDIM
        s1 = _dot_nt(k1, q[:, lo:lo + half].astype(BF16))
        s2 = _dot_nt(k2, q[:, lo + half:lo + PEER_QDIM].astype(BF16))
        return _TopK(s1), _TopK(s2)

    def finish(h, t1, t2, tc):
        _, i1 = t1.result()
        _, i2 = t2.result()
        sc, ci = tc.result()
        ca = jnp.floor(ci * (1.0 / PEER_TOPK))
        cb = ci - ca * PEER_TOPK
        e1 = jnp.zeros_like(ci)
        e2 = jnp.zeros_like(ci)
        for a in range(PEER_TOPK):
            e1 = jnp.where(ca == float(a), i1[a:a + 1], e1)
            e2 = jnp.where(cb == float(a), i2[a:a + 1], e2)
        ex = jnp.exp(sc - sc[0:1])
        gate = ex * (1.0 / jnp.sum(ex, axis=0, keepdims=True))
        rows = slice(h * PEER_TOPK, (h + 1) * PEER_TOPK)
        i1_ref[0, rows, :] = e1
        i2_ref[0, rows, :] = e2
        g_ref[0, rows, :] = gate

    prev = None
    for h in range(PEER_HEADS + 1):
        cur = key_scores(h) if h < PEER_HEADS else None
        for _ in range(PEER_TOPK):
            if cur is not None:
                cur[0].step()
                cur[1].step()
            if prev is not None:
                prev[2].step()
        if prev is not None:
            finish(h - 1, *prev)
        if cur is not None:
            v1, _ = cur[0].result()
            v2, _ = cur[1].result()
            prev = (cur[0], cur[1], _TopK(*_pair_candidates(v1, v2)))


def _peer_sel(x, wq_b, k1_b, k2_b):
    t = x.shape[0]
    tb = LANES
    nblk = t // tb
    nsel = PEER_HEADS * PEER_TOPK
    ospec = pl.BlockSpec((1, nsel, tb), lambda i: (i, 0, 0))
    oshape = jax.ShapeDtypeStruct((nblk, nsel, tb), F32)
    return pl.pallas_call(
        _peer_sel_kernel,
        grid=(nblk,),
        in_specs=[pl.BlockSpec((tb, D_MODEL), lambda i: (i, 0)),
                  pl.BlockSpec((D_MODEL, PEER_HEADS * PEER_QDIM), lambda i: (0, 0)),
                  pl.BlockSpec((N_KEYS, PEER_QDIM // 2), lambda i: (0, 0)),
                  pl.BlockSpec((N_KEYS, PEER_QDIM // 2), lambda i: (0, 0))],
        out_specs=(ospec, ospec, ospec),
        out_shape=(oshape, oshape, oshape),
        compiler_params=_cparams(("parallel",)),
        name="peer_sel",
    )(x, wq_b, k1_b, k2_b)


G_PITCH_PAD = 8
G_TOKENS = 32


def _peer_g_kernel(i1_ref, i2_ref, g_ref, o_ref, i1_scr, i2_scr, g_scr, buf, *, tb):
    pitch = tb + G_PITCH_PAD
    nsel = PEER_HEADS * PEER_TOPK
    i1_scr[...] = i1_ref[0].T
    i2_scr[...] = i2_ref[0].T
    g_scr[...] = g_ref[0].T
    kidx = lax.broadcasted_iota(jnp.int32, (N_KEYS, nsel), 0).astype(F32)

    def body(tg, carry):
        t0 = pl.multiple_of(tg * G_TOKENS, G_TOKENS)
        r1 = i1_scr[pl.ds(t0, G_TOKENS), :]
        r2 = i2_scr[pl.ds(t0, G_TOKENS), :]
        rg = g_scr[pl.ds(t0, G_TOKENS), :]
        ats = [jnp.where(kidx == r1[j:j + 1], rg[j:j + 1], 0.0).astype(BF16) for j in range(G_TOKENS)]
        bts = [jnp.where(kidx == r2[j:j + 1], 1.0, 0.0).astype(BF16) for j in range(G_TOKENS)]
        gts = [_dot_nt(ats[j], bts[j]) for j in range(G_TOKENS)]
        for j in range(G_TOKENS):
            for r in range(N_KEYS // SUBLANES):
                buf[pl.ds(r * SUBLANES * pitch + t0 + j, SUBLANES, stride=pitch), :] = (
                    gts[j][r * SUBLANES:(r + 1) * SUBLANES, :])
        return carry

    lax.fori_loop(0, tb // G_TOKENS, body, 0)
    for i1 in range(N_KEYS):
        o_ref[i1] = buf[i1 * pitch:i1 * pitch + tb, :].astype(BF16)


def _peer_g(i1, i2, g):
    nblk, nsel, tb = i1.shape
    ispec = pl.BlockSpec((1, nsel, tb), lambda i: (i, 0, 0))
    return pl.pallas_call(
        functools.partial(_peer_g_kernel, tb=tb),
        grid=(nblk,),
        in_specs=[ispec, ispec, ispec],
        out_specs=pl.BlockSpec((N_KEYS, tb, N_KEYS), lambda i: (0, i, 0)),
        out_shape=jax.ShapeDtypeStruct((N_KEYS, nblk * tb, N_KEYS), BF16),
        scratch_shapes=[pltpu.VMEM((tb, nsel), F32), pltpu.VMEM((tb, nsel), F32),
                        pltpu.VMEM((tb, nsel), F32),
                        pltpu.VMEM((N_KEYS * (tb + G_PITCH_PAD), N_KEYS), F32)],
        compiler_params=_cparams(("parallel",)),
        name="peer_g",
    )(i1, i2, g)


PEER_I1_PER_CHUNK = 8
PEER_CHUNK = PEER_I1_PER_CHUNK * N_KEYS


PEER_ROW_SPLIT = 2


def _gelu_tanh(x):
    c1 = math.sqrt(2.0 / math.pi)
    t = jnp.tanh(x * (c1 + (c1 * 0.044715) * (x * x)))
    hx = 0.5 * x
    return hx + hx * t


def _peer_ffn_kernel(x_ref, u_ref, v_ref, gm_ref, lg_ref, lb_ref, o_ref, xb_scr, acc_scr):
    c = pl.program_id(1)

    @pl.when(c == 0)
    def _():
        xb_scr[...] = x_ref[...].astype(BF16)
        acc_scr[...] = jnp.zeros_like(acc_scr)

    u = u_ref[...]
    v = v_ref[...]
    tm = xb_scr.shape[0]
    nrow = tm // PEER_ROW_SPLIT

    def hidden(r):
        return _dot_nt(xb_scr[r * nrow:(r + 1) * nrow, :], u)

    def weigh(r, hid):
        ws = []
        for j in range(PEER_I1_PER_CHUNK):
            hj = hid[:, j * N_KEYS:(j + 1) * N_KEYS]
            gj = gm_ref[j, r * nrow:(r + 1) * nrow, :].astype(F32)
            ws.append((_gelu_tanh(hj) * gj).astype(BF16))
        return jnp.concatenate(ws, -1)

    hids = [hidden(0)]
    for r in range(PEER_ROW_SPLIT):
        if r + 1 < PEER_ROW_SPLIT:
            hids.append(hidden(r + 1))
        w = weigh(r, hids[r])
        acc_scr[r * nrow:(r + 1) * nrow, :] += _dot(w, v)

    @pl.when(c == pl.num_programs(1) - 1)
    def _():
        o_ref[...] = _layer_norm(DEEPNORM_ALPHA * x_ref[...] + acc_scr[...], lg_ref[...], lb_ref[...])


def _peer_ffn(x, u_b, v_b, gmap, lg, lb, tm):
    t = x.shape[0]
    nchunk = u_b.shape[0] // PEER_CHUNK
    return pl.pallas_call(
        _peer_ffn_kernel,
        grid=(t // tm, nchunk),
        in_specs=[pl.BlockSpec((tm, D_MODEL), lambda i, c: (i, 0)),
                  pl.BlockSpec((PEER_CHUNK, D_MODEL), lambda i, c: (c, 0)),
                  pl.BlockSpec((PEER_CHUNK, D_MODEL), lambda i, c: (c, 0)),
                  pl.BlockSpec((PEER_I1_PER_CHUNK, tm, N_KEYS), lambda i, c: (c, i, 0)),
                  pl.BlockSpec((1, D_MODEL), lambda i, c: (0, 0)),
                  pl.BlockSpec((1, D_MODEL), lambda i, c: (0, 0))],
        out_specs=pl.BlockSpec((tm, D_MODEL), lambda i, c: (i, 0)),
        out_shape=jax.ShapeDtypeStruct((t, D_MODEL), F32),
        scratch_shapes=[pltpu.VMEM((tm, D_MODEL), BF16), pltpu.VMEM((tm, D_MODEL), F32)],
        compiler_params=_cparams(("parallel", "arbitrary")),
        name="peer_ffn",
    )(x, u_b, v_b, gmap, lg, lb)


def _peer(x2, wts, tm):
    i1, i2, g = _peer_sel(x2, wts["w_pq"], wts["keys1"], wts["keys2"])
    gmap = _peer_g(i1, i2, g)
    return _peer_ffn(x2, wts["peer_u"], wts["peer_v"], gmap, wts["ln3_g"], wts["ln3_b"], tm)


def _rope_tables(pos):
    half = ATT_HEAD_DIM // 2
    inv_freq = ROPE_THETA ** (-jnp.arange(half, dtype=F32) / half)
    ang = pos.astype(F32)[:, None] * inv_freq[None, :]
    cos = jnp.cos(ang)
    sin = jnp.sin(ang)
    reps = LANES // ATT_HEAD_DIM
    cos_t = jnp.tile(jnp.concatenate([cos, cos], -1), (1, reps))
    sin_t = jnp.tile(jnp.concatenate([-sin, sin], -1), (1, reps))
    return cos_t, sin_t


def _prep_weights(w_in, w_conv, dn_a_log, dn_dt_bias, dn_norm_g, attn_sinks, w_o, ln1_g, ln1_b,
                  w_cq, w_ck, w_cv, w_co, ln2_g, ln2_b, w_pq, peer_keys1, peer_keys2,
                  peer_u, peer_v, ln3_g, ln3_b):
    o_dqkv = ATT_Q + 2 * ATT_KV
    o_ba = o_dqkv + CONV_CH
    o_dz = o_ba + 2 * DN_HEADS
    w_ba = w_in[:, o_ba:o_dz]
    w_p = jnp.concatenate([w_in[:, :o_ba], w_in[:, o_dz:], w_ba,
                           jnp.zeros((D_MODEL, LANES - 2 * DN_HEADS), F32)], -1).astype(BF16)
    lane_par = lambda p: jnp.zeros((LANES,), F32).at[DN_HEADS:2 * DN_HEADS].set(p)
    par_c = jnp.zeros((SUBLANES, LANES), F32).at[0].set(lane_par(dn_a_log)).at[1].set(lane_par(dn_dt_bias))
    row_par = lambda p: jnp.broadcast_to(
        jnp.zeros((SUBLANES,), F32).at[DN_HEADS:2 * DN_HEADS].set(p)[:, None], (SUBLANES, LANES))
    par_r = jnp.stack([row_par(dn_a_log), row_par(dn_dt_bias)])
    row = lambda p: p.reshape(1, -1)
    return dict(
        w_in=w_p, w_ba_t=w_ba.T.astype(BF16), w_conv=w_conv, par_c=par_c, par_r=par_r,
        norm_g=row(dn_norm_g), sinks=attn_sinks,
        w_o_a=w_o[:ATT_Q].astype(BF16), w_o_d=w_o[ATT_Q:].astype(BF16),
        ln1_g=row(ln1_g), ln1_b=row(ln1_b),
        w_cq=w_cq.astype(BF16), w_ck=w_ck.astype(BF16), w_cv=w_cv.astype(BF16), w_co=w_co.astype(BF16),
        ln2_g=row(ln2_g), ln2_b=row(ln2_b),
        w_pq=w_pq.reshape(D_MODEL, PEER_HEADS * PEER_QDIM).astype(BF16),
        keys1=peer_keys1.astype(BF16), keys2=peer_keys2.astype(BF16),
        peer_u=peer_u.astype(BF16), peer_v=peer_v.astype(BF16),
        ln3_g=row(ln3_g), ln3_b=row(ln3_b),
    )


def _tile(t, pref):
    while t % pref:
        pref //= 2
    return pref


XATTN_SEQS = 4
SWA_SEQS = 8
TOKEN_TILE = 512
PEER_TILE = 1024


def _layer_tail(xf, a_out, dn_out, mk, mv, nb, seq, wts):
    t = xf.shape[0]
    tm = _tile(t, TOKEN_TILE)
    x1, qc = _mm_ln([a_out, dn_out], [wts["w_o_a"], wts["w_o_d"]], xf, wts["ln1_g"], wts["ln1_b"], tm,
                    wts["w_cq"])
    tq = _tile(seq, 512)
    nblk = seq // tq
    xo = _xattn(qc, mk, mv, nb, nblk, tq, _tile(nb, XATTN_SEQS) if nblk == 1 else 1)
    x2 = _mm_ln([xo], [wts["w_co"]], x1, wts["ln2_g"], wts["ln2_b"], tm)
    return _peer(x2, wts, _tile(t, PEER_TILE))


def _prompt_layer(x, mem, wts):
    nb, seq, _ = x.shape
    t = nb * seq
    xf = x.reshape(t, D_MODEL)
    cos, sin = _rope_tables(jnp.arange(seq, dtype=jnp.int32))
    q, k, v, dqkv, dz, ba, bat = _in_proj(xf, wts["w_in"], wts["w_ba_t"], cos, sin, _tile(seq, TOKEN_TILE))
    nblk = seq // WINDOW
    a_out = _swa(wts["sinks"], q, k, v, nb, nblk, WINDOW, WINDOW, True, 1)
    c = DN_CHUNK if seq % DN_CHUNK == 0 else seq
    nchunk = seq // c
    bat_c = bat.reshape(SUBLANES, nb * nchunk, c).transpose(1, 0, 2)
    tail = jnp.zeros((nb * SUBLANES, CONV_CH), F32)
    s0 = jnp.zeros((nb, DN_HEADS, DN_HEAD_DIM, DN_HEAD_DIM), F32)
    dn_out, s_new = _deltanet(dqkv, tail, wts["w_conv"], ba, bat_c, dz, s0, wts["par_c"], wts["par_r"],
                              wts["norm_g"], nb, nchunk, c, c)
    memf = mem.reshape(nb * MEM_LEN, D_MODEL)
    mk = _mm(memf, wts["w_ck"], _tile(nb * MEM_LEN, 256))
    mv = _mm(memf, wts["w_cv"], _tile(nb * MEM_LEN, 256))
    y = _layer_tail(xf, a_out, dn_out, mk, mv, nb, seq, wts)
    new_k = k.reshape(nb, seq, ATT_KV_HEADS, ATT_HEAD_DIM)[:, -WINDOW:]
    new_v = v.reshape(nb, seq, ATT_KV_HEADS, ATT_HEAD_DIM)[:, -WINDOW:]
    new_conv = dqkv.reshape(nb, seq, CONV_CH)[:, -(CONV_WIDTH - 1):]
    mk4 = mk.reshape(nb, MEM_LEN, X_HEADS, X_HEAD_DIM)
    mv4 = mv.reshape(nb, MEM_LEN, X_HEADS, X_HEAD_DIM)
    return y.reshape(nb, seq, D_MODEL), new_k, new_v, s_new, new_conv, mk4, mv4


def _sample_layer(x, past_k, past_v, s0, conv_buf, mem_k, mem_v, wts):
    nb, seq, _ = x.shape
    assert CONV_WIDTH - 1 <= seq <= SAMPLE_PAD
    pad = SAMPLE_PAD
    t = nb * pad
    xf = jnp.pad(x, ((0, 0), (0, pad - seq), (0, 0))).reshape(t, D_MODEL)
    cos8, sin8 = _rope_tables(PAST_LEN + jnp.arange(pad, dtype=jnp.int32))
    tm = _tile(t, TOKEN_TILE)
    cos = jnp.tile(cos8, (tm // pad, 1))
    sin = jnp.tile(sin8, (tm // pad, 1))
    q, k, v, dqkv, dz, ba, bat = _in_proj(xf, wts["w_in"], wts["w_ba_t"], cos, sin, tm)
    npast = past_k.shape[1]
    kk = jnp.concatenate([past_k.reshape(nb, npast, ATT_KV), k.reshape(nb, pad, ATT_KV)], 1)
    vv = jnp.concatenate([past_v.reshape(nb, npast, ATT_KV), v.reshape(nb, pad, ATT_KV)], 1)
    nk = npast + pad
    a_out = _swa(wts["sinks"], q, kk.reshape(nb * nk, ATT_KV), vv.reshape(nb * nk, ATT_KV),
                 nb, 1, pad, nk, False, _tile(nb, SWA_SEQS))
    bat_c = bat.reshape(SUBLANES, nb, pad).transpose(1, 0, 2)
    tail = jnp.pad(conv_buf, ((0, 0), (SUBLANES - (CONV_WIDTH - 1), 0), (0, 0))).reshape(nb * SUBLANES, CONV_CH)
    dn_out, s_new = _deltanet(dqkv, tail, wts["w_conv"], ba, bat_c, dz, s0, wts["par_c"], wts["par_r"],
                              wts["norm_g"], nb, 1, pad, seq)
    mk = mem_k.reshape(nb * MEM_LEN, D_MODEL)
    mv = mem_v.reshape(nb * MEM_LEN, D_MODEL)
    y = _layer_tail(xf, a_out, dn_out, mk, mv, nb, pad, wts)
    new_k = kk[:, seq:seq + npast].reshape(nb, npast, ATT_KV_HEADS, ATT_HEAD_DIM)
    new_v = vv[:, seq:seq + npast].reshape(nb, npast, ATT_KV_HEADS, ATT_HEAD_DIM)
    new_conv = dqkv.reshape(nb, pad, CONV_CH)[:, seq - (CONV_WIDTH - 1):seq]
    return y.reshape(nb, pad, D_MODEL)[:, :seq], new_k, new_v, s_new, new_conv


def kernel(x_prompt, x_sample, cache_swa_k, cache_swa_v, state_dn, state_dn_conv, cache_mem_k, cache_mem_v, mem_prompt, w_in, w_conv, dn_a_log, dn_dt_bias, dn_norm_g, attn_sinks, w_o, ln1_g, ln1_b, w_cq, w_ck, w_cv, w_co, ln2_g, ln2_b, w_pq, peer_keys1, peer_keys2, peer_u, peer_v, ln3_g, ln3_b):
    assert w_in.shape[0] == DEPTH == 1
    wts = _prep_weights(w_in[0], w_conv[0], dn_a_log[0], dn_dt_bias[0], dn_norm_g[0], attn_sinks[0],
                        w_o[0], ln1_g[0], ln1_b[0], w_cq[0], w_ck[0], w_cv[0], w_co[0], ln2_g[0], ln2_b[0],
                        w_pq[0], peer_keys1[0], peer_keys2[0], peer_u[0], peer_v[0], ln3_g[0], ln3_b[0])
    y_p, pk, pv, ps, pc, pmk, pmv = _prompt_layer(x_prompt, mem_prompt, wts)
    y_s, sk, sv, ss, sc = _sample_layer(x_sample, cache_swa_k[0], cache_swa_v[0], state_dn[0],
                                        state_dn_conv[0], cache_mem_k[0], cache_mem_v[0], wts)
    st = lambda a: a[None]
    return (y_p, y_s, st(pk), st(pv), st(ps), st(pc), st(pmk), st(pmv), st(sk), st(sv), st(ss), st(sc))
```

```python
import functools
import math

import jax
import jax.numpy as jnp
import numpy as np
from jax import lax
from jax.experimental import pallas as pl
from jax.experimental.pallas import tpu as pltpu

F32 = jnp.float32
BF16 = jnp.bfloat16

D_MODEL = 1024
DEPTH = 1
PAST_LEN = 16384
ATT_HEADS = 8
ATT_KV_HEADS = 2
ATT_HEAD_DIM = 64
ATT_GROUP = ATT_HEADS // ATT_KV_HEADS
WINDOW = 128
ROPE_THETA = 10000.0
DN_HEADS = 4
DN_HEAD_DIM = 128
CONV_WIDTH = 4
DN_CHUNK = 64
MEM_LEN = 256
X_HEADS = 4
X_HEAD_DIM = D_MODEL // X_HEADS
N_KEYS = 128
PEER_HEADS = 8
PEER_QDIM = 256
PEER_TOPK = 16
LN_EPS = 1e-5
RMS_EPS = 1e-6
NEG_INF = -1e30
DEEPNORM_ALPHA = (2 * DEPTH) ** 0.25

ATT_Q = ATT_HEADS * ATT_HEAD_DIM
ATT_KV = ATT_KV_HEADS * ATT_HEAD_DIM
DN_W = DN_HEADS * DN_HEAD_DIM
CONV_CH = 3 * DN_W
IN_PACKED = ATT_Q + 2 * ATT_KV + CONV_CH + DN_W + 128

LANES = 128
SUBLANES = 8
VMEM_LIMIT = 48 * 1024 * 1024
SAMPLE_PAD = 8


def _cparams(sem):
    return pltpu.CompilerParams(dimension_semantics=sem, vmem_limit_bytes=VMEM_LIMIT)


def _dot(a, b):
    return jnp.dot(a, b, preferred_element_type=F32)


def _dot_nt(a, b):
    return lax.dot_general(a, b, (((1,), (1,)), ((), ())), preferred_element_type=F32)


def _dot_tn(a, b):
    return lax.dot_general(a, b, (((0,), (0,)), ((), ())), preferred_element_type=F32)


def _split3(x):
    x0 = x.astype(BF16)
    r = x - x0.astype(F32)
    x1 = r.astype(BF16)
    x2 = (r - x1.astype(F32)).astype(BF16)
    return x0, x1, x2


def _dot_exact_lhs(a_bf16, b):
    b0, b1, b2 = _split3(b)
    return _dot(a_bf16, b0) + _dot(a_bf16, b1) + _dot(a_bf16, b2)


def _dot_exact_rhs(a, b_bf16):
    a0, a1, a2 = _split3(a)
    return _dot(a0, b_bf16) + _dot(a1, b_bf16) + _dot(a2, b_bf16)


def _dot_hi(a, b):
    a0 = a.astype(BF16)
    a1 = (a - a0.astype(F32)).astype(BF16)
    b0 = b.astype(BF16)
    b1 = (b - b0.astype(F32)).astype(BF16)
    return _dot(a0, b0) + _dot(a0, b1) + _dot(a1, b0)


def _sigmoid(x):
    return 1.0 / (1.0 + jnp.exp(-x))


def _layer_norm(y, g, b):
    mu = jnp.mean(y, -1, keepdims=True)
    d = y - mu
    var = jnp.mean(d * d, -1, keepdims=True)
    return d * lax.rsqrt(var + LN_EPS) * g + b


def _in_proj_kernel(x_ref, w_ref, wt_ref, cos_ref, sin_ref,
                    q_ref, k_ref, v_ref, dqkv_ref, dz_ref, ba_ref, bat_ref):
    xb = x_ref[...].astype(BF16)
    h = _dot(xb, w_ref[...])
    cos = cos_ref[...]
    sin = sin_ref[...]
    lane = lax.broadcasted_iota(jnp.int32, cos.shape, 1)
    first = (lane & (ATT_HEAD_DIM - 1)) < (ATT_HEAD_DIM // 2)

    def rope(t):
        up = pltpu.roll(t, LANES - ATT_HEAD_DIM // 2, 1)
        dn = pltpu.roll(t, ATT_HEAD_DIM // 2, 1)
        return t * cos + jnp.where(first, up, dn) * sin

    for j in range(ATT_Q // LANES):
        q_ref[:, j * LANES:(j + 1) * LANES] = rope(h[:, j * LANES:(j + 1) * LANES]).astype(q_ref.dtype)
    k_ref[...] = rope(h[:, ATT_Q:ATT_Q + ATT_KV])
    o = ATT_Q + ATT_KV
    v_ref[...] = h[:, o:o + ATT_KV]
    o += ATT_KV
    dqkv_ref[...] = h[:, o:o + CONV_CH]
    o += CONV_CH
    dz_ref[...] = h[:, o:o + DN_W]
    o += DN_W
    ba_ref[...] = h[:, o:o + LANES]
    bat_ref[...] = _dot_nt(wt_ref[...], xb)


def _in_proj(x, w_p, w_t, cos, sin, tm):
    t = x.shape[0]
    nt = t // tm
    ncos = cos.shape[0] // tm
    row = lambda width: pl.BlockSpec((tm, width), lambda i: (i, 0))
    out_shape = (
        jax.ShapeDtypeStruct((t, ATT_Q), BF16),
        jax.ShapeDtypeStruct((t, ATT_KV), F32),
        jax.ShapeDtypeStruct((t, ATT_KV), F32),
        jax.ShapeDtypeStruct((t, CONV_CH), F32),
        jax.ShapeDtypeStruct((t, DN_W), F32),
        jax.ShapeDtypeStruct((t, LANES), F32),
        jax.ShapeDtypeStruct((SUBLANES, t), F32),
    )
    return pl.pallas_call(
        _in_proj_kernel,
        grid=(nt,),
        in_specs=[
            row(D_MODEL),
            pl.BlockSpec((D_MODEL, IN_PACKED), lambda i: (0, 0)),
            pl.BlockSpec((SUBLANES, D_MODEL), lambda i: (0, 0)),
            pl.BlockSpec((tm, LANES), lambda i: (i % ncos, 0)),
            pl.BlockSpec((tm, LANES), lambda i: (i % ncos, 0)),
        ],
        out_specs=(row(ATT_Q), row(ATT_KV), row(ATT_KV), row(CONV_CH), row(DN_W), row(LANES),
                   pl.BlockSpec((SUBLANES, tm), lambda i: (0, i))),
        out_shape=out_shape,
        compiler_params=_cparams(("parallel",)),
        name="in_proj",
    )(x, w_p, w_t, cos, sin)


def _swa_kernel(sinks_ref, q_ref, *refs, has_prev, units):
    if has_prev:
        kp_ref, kc_ref, vp_ref, vc_ref, o_ref = refs
        k = jnp.concatenate([kp_ref[...], kc_ref[...]], 0)
        v = jnp.concatenate([vp_ref[...], vc_ref[...]], 0)
    else:
        kc_ref, vc_ref, o_ref = refs
        k = kc_ref[...]
        v = vc_ref[...]
    q = q_ref[...].astype(F32)
    r, nk = q.shape[0] // units, k.shape[0] // units
    gsz = ATT_GROUP if r < LANES else 1
    ngrp = ATT_HEADS // gsz
    rows = gsz * r
    qi = lax.broadcasted_iota(jnp.int32, (rows, nk), 0) & (r - 1)
    kj = lax.broadcasted_iota(jnp.int32, (rows, nk), 1)
    valid = (kj >= qi) & (kj <= qi + WINDOW)
    if has_prev:
        jmin = jnp.where(pl.program_id(1) == 0, WINDOW, 0)
        valid = valid & (kj >= jmin)
    ug = [(u, g) for u in range(units) for g in range(ngrp)]
    kvh = lambda g: (g * gsz) // ATT_GROUP
    hd = lambda h: slice(h * ATT_HEAD_DIM, (h + 1) * ATT_HEAD_DIM)
    kb = {x: k[x[0] * nk:(x[0] + 1) * nk, hd(kvh(x[1]))].astype(BF16) for x in ug}
    vb = {x: v[x[0] * nk:(x[0] + 1) * nk, hd(kvh(x[1]))].astype(BF16) for x in ug}
    if gsz == 1:
        sink = {g: sinks_ref[g] for g in range(ngrp)}
    else:
        sink = {g: jnp.concatenate([jnp.full((r, 1), sinks_ref[g * gsz + i], F32) for i in range(gsz)], 0)
                for g in range(ngrp)}
    qs = {x: jnp.concatenate([q[x[0] * r:(x[0] + 1) * r, hd(x[1] * gsz + i)] for i in range(gsz)], 0).astype(BF16)
          for x in ug}
    s = {x: _dot_nt(qs[x], kb[x]) * (ATT_HEAD_DIM ** -0.5) for x in ug}
    p = {}
    for x in ug:
        sx = jnp.where(valid, s[x], NEG_INF)
        m = jnp.maximum(jnp.max(sx, -1, keepdims=True), sink[x[1]])
        e = jnp.exp(sx - m)
        den = jnp.sum(e, -1, keepdims=True) + jnp.exp(sink[x[1]] - m)
        p[x] = (e * (1.0 / den)).astype(BF16)
    o = {x: _dot(p[x], vb[x]) for x in ug}
    outs = []
    for u in range(units):
        outs.append(jnp.concatenate([o[(u, h // gsz)][(h % gsz) * r:(h % gsz + 1) * r]
                                     for h in range(ATT_HEADS)], -1))
    o_ref[...] = jnp.concatenate(outs, 0).astype(o_ref.dtype)


def _swa(sinks, q, k, v, nb, nblk, rq, nk, has_prev, units):
    assert rq & (rq - 1) == 0 and nb % units == 0 and (units == 1 or nblk == 1)
    smem = pl.BlockSpec(memory_space=pltpu.SMEM)
    qspec = pl.BlockSpec((units * rq, ATT_Q), lambda b, n: (b * nblk + n, 0))
    cur = pl.BlockSpec((units * nk, ATT_KV), lambda b, n: (b * nblk + n, 0))
    if has_prev:
        prev = pl.BlockSpec((nk, ATT_KV), lambda b, n: (b * nblk + jnp.maximum(n - 1, 0), 0))
        in_specs = [smem, qspec, prev, cur, prev, cur]
        args = (sinks, q, k, k, v, v)
    else:
        in_specs = [smem, qspec, cur, cur]
        args = (sinks, q, k, v)
    return pl.pallas_call(
        functools.partial(_swa_kernel, has_prev=has_prev, units=units),
        grid=(nb // units, nblk),
        in_specs=in_specs,
        out_specs=qspec,
        out_shape=jax.ShapeDtypeStruct(q.shape, BF16),
        compiler_params=_cparams(("parallel", "arbitrary")),
        name="swa_prev" if has_prev else "swa_cache",
    )(*args)


def _deltanet_kernel(x_ref, tail_ref, wc_ref, ba_ref, bat_ref, dz_ref, s0_ref,
                     pc_ref, pr_ref, ng_ref, o_ref, s_ref, s_scr, xp_scr, *, c, c_real, nsq, group, chained):
    n = pl.program_id(1)
    rows = group * c
    wc = wc_ref[...]
    base = SUBLANES - (CONV_WIDTH - 1)

    def conv(lo, nrow):
        acc = xp_scr[lo + base:lo + base + nrow] * wc[0:1]
        for j in range(1, CONV_WIDTH):
            acc = acc + xp_scr[lo + base + j:lo + base + j + nrow] * wc[j:j + 1]
        return acc

    if chained:
        @pl.when(n == 0)
        def _():
            s_scr[...] = s0_ref[0]
            xp_scr[0:SUBLANES] = tail_ref[...]

        xp_scr[SUBLANES:SUBLANES + rows] = x_ref[...]
        y = conv(0, rows)
        xp_scr[0:SUBLANES] = xp_scr[rows:rows + SUBLANES]
    else:
        seg = SUBLANES + c
        ys = []
        for u in range(group):
            xp_scr[u * seg:u * seg + SUBLANES] = tail_ref[u * SUBLANES:(u + 1) * SUBLANES]
            xp_scr[u * seg + SUBLANES:(u + 1) * seg] = x_ref[u * c:(u + 1) * c]
            ys.append(conv(u * seg, c))
        y = jnp.concatenate(ys, 0)
    y = y * _sigmoid(y)

    ba = ba_ref[...]
    bat = jnp.concatenate([bat_ref[u] for u in range(group)], -1)
    beta_c = _sigmoid(ba)
    g_c = -jnp.exp(pc_ref[0:1]) * _softplus(ba + pc_ref[1:2])
    beta_r = _sigmoid(bat)
    g_r = -jnp.exp(pr_ref[0][:, 0:1]) * _softplus(bat + pr_ref[1][:, 0:1])
    if c_real < c:
        rmask = (lax.broadcasted_iota(jnp.int32, ba.shape, 0) & (c - 1)) < c_real
        beta_c = jnp.where(rmask, beta_c, 0.0)
        g_c = jnp.where(rmask, g_c, 0.0)
        lmask = (lax.broadcasted_iota(jnp.int32, bat.shape, 1) & (c - 1)) < c_real
        beta_r = jnp.where(lmask, beta_r, 0.0)
        g_r = jnp.where(lmask, g_r, 0.0)

    shift = int(math.log2(c))
    bi = lax.broadcasted_iota(jnp.int32, (rows, rows), 0)
    bj = lax.broadcasted_iota(jnp.int32, (rows, rows), 1)
    same = (bi >> shift) == (bj >> shift)
    tri = jnp.where(same & (bi >= bj), 1.0, 0.0).astype(BF16)
    tri_t = jnp.where(same & (bi <= bj), 1.0, 0.0).astype(BF16)
    gc_c = _dot_exact_lhs(tri, g_c)
    gc_r = _dot_exact_rhs(g_r, tri_t)
    ii = lax.broadcasted_iota(jnp.int32, (c, c), 0)
    jj = lax.broadcasted_iota(jnp.int32, (c, c), 1)
    eye = jnp.where(ii == jj, 1.0, 0.0)
    incl = ii >= jj
    strict = ii > jj

    ng = ng_ref[...]
    dz = dz_ref[...]
    hs = range(DN_HEADS)
    uh = [(u, h) for u in range(group) for h in hs]
    rs = lambda u: slice(u * c, (u + 1) * c)
    sl = lambda base, h: slice(base + h * DN_HEAD_DIM, base + (h + 1) * DN_HEAD_DIM)
    l2 = lambda t: t * lax.rsqrt(jnp.sum(t * t, -1, keepdims=True) + RMS_EPS)
    q = {x: l2(y[rs(x[0]), sl(0, x[1])]) * (DN_HEAD_DIM ** -0.5) for x in uh}
    k = {x: l2(y[rs(x[0]), sl(DN_W, x[1])]) for x in uh}
    v = {x: y[rs(x[0]), sl(2 * DN_W, x[1])] for x in uh}
    beta = {x: beta_c[rs(x[0]), x[1]:x[1] + 1] for x in uh}
    gcc = {x: gc_c[rs(x[0]), DN_HEADS + x[1]:DN_HEADS + x[1] + 1] for x in uh}
    gcr = {x: gc_r[DN_HEADS + x[1]:DN_HEADS + x[1] + 1, rs(x[0])] for x in uh}
    decay = {x: jnp.exp(jnp.where(incl, gcc[x] - gcr[x], NEG_INF)) for x in uh}
    kbeta = {x: k[x] * beta[x] for x in uh}
    kq = {x: k[x].astype(BF16) for x in uh}
    a = {x: jnp.where(strict, _dot_nt(kbeta[x].astype(BF16), kq[x]) * decay[x], 0.0) for x in uh}
    qk = {x: (_dot_nt(q[x].astype(BF16), kq[x]) * decay[x]).astype(BF16) for x in uh}
    p = {x: eye - a[x] for x in uh}
    m = a
    for _ in range(nsq):
        m = {x: _dot_hi(m[x], m[x]) for x in uh}
        p = {x: p[x] + _dot_hi(p[x], m[x]) for x in uh}
    sol = {x: _dot_hi(p[x], jnp.concatenate([v[x] * beta[x], kbeta[x] * jnp.exp(gcc[x])], -1)) for x in uh}
    g_last = {x: gcr[x][:, c - 1:c] for x in uh}
    qg = {x: (q[x] * jnp.exp(gcc[x])).astype(BF16) for x in uh}
    kd = {x: (k[x] * jnp.exp(g_last[x] - gcc[x])).astype(BF16) for x in uh}

    def advance(xs, s_old):
        s_b = {x: s_old[x].astype(BF16) for x in xs}
        v_nb = {x: (sol[x][:, :DN_HEAD_DIM] - _dot(sol[x][:, DN_HEAD_DIM:].astype(BF16), s_b[x])).astype(BF16)
                for x in xs}
        s_new = {x: s_old[x] * jnp.exp(g_last[x]) + _dot_tn(kd[x], v_nb[x]) for x in xs}
        o = {x: _dot(qg[x], s_b[x]) + _dot(qk[x], v_nb[x]) for x in xs}
        return s_new, o

    o_all = {}
    if chained:
        s_cur = {h: s_scr[h] for h in hs}
        for u in range(group):
            xs = [(u, h) for h in hs]
            s_new, o = advance(xs, {x: s_cur[x[1]] for x in xs})
            s_cur = {h: s_new[(u, h)] for h in hs}
            o_all.update(o)
        for h in hs:
            s_scr[h] = s_cur[h]
    else:
        s_new, o_all = advance(uh, {x: s0_ref[x[0], x[1]] for x in uh})
        for x in uh:
            s_ref[x[0], x[1]] = s_new[x]

    outs = []
    for u in range(group):
        parts = []
        for h in hs:
            o = o_all[(u, h)]
            o = o * lax.rsqrt(jnp.mean(o * o, -1, keepdims=True) + RMS_EPS) * ng
            z = dz[rs(u), sl(0, h)]
            parts.append(o * (z * _sigmoid(z)))
        outs.append(jnp.concatenate(parts, -1))
    o_ref[...] = jnp.concatenate(outs, 0).astype(o_ref.dtype)

    if chained:
        @pl.when(n == pl.num_programs(1) - 1)
        def _():
            s_ref[0] = s_scr[...]


def _softplus(x):
    return jnp.maximum(x, 0.0) + jnp.log(1.0 + jnp.exp(-jnp.abs(x)))


DN_GROUP_CHUNKS = 4
DN_GROUP_SEQS = 8


def _deltanet(dqkv, tail, w_conv, ba, bat, dz, s0, par_c, par_r, norm_g, nb, nchunk, c, c_real):
    assert c & (c - 1) == 0
    nsq = max(int(math.ceil(math.log2(c))) - 1, 0)
    chained = nchunk > 1
    group = _tile(nchunk, DN_GROUP_CHUNKS) if chained else _tile(nb, DN_GROUP_SEQS)
    nstep = nchunk // group if chained else 1
    ngrid0 = nb if chained else nb // group
    nstate = 1 if chained else group
    ntail = SUBLANES if chained else group * SUBLANES
    rows = lambda width: pl.BlockSpec((group * c, width), lambda b, n: (b * nstep + n, 0))
    const2 = lambda shape: pl.BlockSpec(shape, lambda b, n: (0,) * len(shape))
    sspec = pl.BlockSpec((nstate, DN_HEADS, DN_HEAD_DIM, DN_HEAD_DIM), lambda b, n: (b, 0, 0, 0))
    xp_rows = SUBLANES + group * c if chained else group * (SUBLANES + c)
    return pl.pallas_call(
        functools.partial(_deltanet_kernel, c=c, c_real=c_real, nsq=nsq, group=group, chained=chained),
        grid=(ngrid0, nstep),
        in_specs=[
            rows(CONV_CH),
            pl.BlockSpec((ntail, CONV_CH), lambda b, n: (b, 0)),
            const2((CONV_WIDTH, CONV_CH)),
            rows(LANES),
            pl.BlockSpec((group, SUBLANES, c), lambda b, n: (b * nstep + n, 0, 0)),
            rows(DN_W),
            sspec,
            const2((SUBLANES, LANES)),
            const2((2, SUBLANES, LANES)),
            const2((1, DN_HEAD_DIM)),
        ],
        out_specs=(rows(DN_W), sspec),
        out_shape=(jax.ShapeDtypeStruct((nb * nchunk * c, DN_W), BF16),
                   jax.ShapeDtypeStruct((nb, DN_HEADS, DN_HEAD_DIM, DN_HEAD_DIM), F32)),
        scratch_shapes=[pltpu.VMEM((DN_HEADS, DN_HEAD_DIM, DN_HEAD_DIM), F32),
                        pltpu.VMEM((xp_rows, CONV_CH), F32)],
        compiler_params=_cparams(("parallel", "arbitrary")),
        name="deltanet_c%d" % c,
    )(dqkv, tail, w_conv, ba, bat, dz, s0, par_c, par_r, norm_g)


def _mm_kernel(x_ref, w_ref, o_ref):
    o_ref[...] = _dot(x_ref[...].astype(BF16), w_ref[...]).astype(o_ref.dtype)


def _mm(x, w_b, tm, out_dtype=F32):
    t, kdim = x.shape
    n = w_b.shape[1]
    return pl.pallas_call(
        _mm_kernel,
        grid=(t // tm,),
        in_specs=[pl.BlockSpec((tm, kdim), lambda i: (i, 0)),
                  pl.BlockSpec((kdim, n), lambda i: (0, 0))],
        out_specs=pl.BlockSpec((tm, n), lambda i: (i, 0)),
        out_shape=jax.ShapeDtypeStruct((t, n), out_dtype),
        compiler_params=_cparams(("parallel",)),
        name="mm",
    )(x, w_b)


def _mm_ln_kernel(*refs, n_lhs, has_next):
    lhs = refs[:n_lhs]
    ws = refs[n_lhs:2 * n_lhs]
    rest = refs[2 * n_lhs:]
    res_ref, g_ref, b_ref = rest[:3]
    acc = _dot(lhs[0][...].astype(BF16), ws[0][...])
    for a_ref, w_ref in zip(lhs[1:], ws[1:]):
        acc = acc + _dot(a_ref[...].astype(BF16), w_ref[...])
    y = _layer_norm(DEEPNORM_ALPHA * res_ref[...] + acc, g_ref[...], b_ref[...])
    if has_next:
        wn_ref, o_ref, on_ref = rest[3:]
        on_ref[...] = _dot(y.astype(BF16), wn_ref[...]).astype(on_ref.dtype)
    else:
        (o_ref,) = rest[3:]
    o_ref[...] = y


def _mm_ln(lhs, ws, resid, g, b, tm, w_next=None):
    t = resid.shape[0]
    n_lhs = len(lhs)
    row = pl.BlockSpec((tm, D_MODEL), lambda i: (i, 0))
    in_specs = [pl.BlockSpec((tm, a.shape[1]), lambda i: (i, 0)) for a in lhs]
    in_specs += [pl.BlockSpec(w.shape, lambda i: (0, 0)) for w in ws]
    in_specs += [row, pl.BlockSpec((1, D_MODEL), lambda i: (0, 0)), pl.BlockSpec((1, D_MODEL), lambda i: (0, 0))]
    args = [*lhs, *ws, resid, g, b]
    out_specs, out_shape = row, jax.ShapeDtypeStruct((t, D_MODEL), F32)
    if w_next is not None:
        in_specs.append(pl.BlockSpec(w_next.shape, lambda i: (0, 0)))
        args.append(w_next)
        out_specs = (row, pl.BlockSpec((tm, w_next.shape[1]), lambda i: (i, 0)))
        out_shape = (out_shape, jax.ShapeDtypeStruct((t, w_next.shape[1]), BF16))
    return pl.pallas_call(
        functools.partial(_mm_ln_kernel, n_lhs=n_lhs, has_next=w_next is not None),
        grid=(t // tm,),
        in_specs=in_specs,
        out_specs=out_specs,
        out_shape=out_shape,
        compiler_params=_cparams(("parallel",)),
        name="mm_ln%d" % n_lhs,
    )(*args)


def _xattn_kernel(q_ref, k_ref, v_ref, o_ref, *, units):
    q = q_ref[...].astype(F32)
    tq = q.shape[0] // units
    uh = [(u, h) for u in range(units) for h in range(X_HEADS)]
    hd = lambda h: slice(h * X_HEAD_DIM, (h + 1) * X_HEAD_DIM)
    mem = lambda ref, x: ref[x[0] * MEM_LEN:(x[0] + 1) * MEM_LEN, hd(x[1])].astype(BF16)
    qh = {x: q[x[0] * tq:(x[0] + 1) * tq, hd(x[1])].astype(BF16) for x in uh}
    s = {x: _dot_nt(qh[x], mem(k_ref, x)) * (X_HEAD_DIM ** -0.5) for x in uh}
    p = {}
    for x in uh:
        e = jnp.exp(s[x] - jnp.max(s[x], -1, keepdims=True))
        p[x] = (e * (1.0 / jnp.sum(e, -1, keepdims=True))).astype(BF16)
    o = {x: _dot(p[x], mem(v_ref, x)) for x in uh}
    outs = [jnp.concatenate([o[(u, h)] for h in range(X_HEADS)], -1) for u in range(units)]
    o_ref[...] = jnp.concatenate(outs, 0).astype(o_ref.dtype)


def _xattn(q, mk, mv, nb, nblk, tq, units):
    assert nb % units == 0 and (units == 1 or nblk == 1)
    qspec = pl.BlockSpec((units * tq, D_MODEL), lambda b, i: (b * nblk + i, 0))
    mspec = pl.BlockSpec((units * MEM_LEN, D_MODEL), lambda b, i: (b, 0))
    return pl.pallas_call(
        functools.partial(_xattn_kernel, units=units),
        grid=(nb // units, nblk),
        in_specs=[qspec, mspec, mspec],
        out_specs=qspec,
        out_shape=jax.ShapeDtypeStruct(q.shape, BF16),
        compiler_params=_cparams(("parallel", "arbitrary")),
        name="xattn",
    )(q, mk, mv)


class _TopK:
    def __init__(self, parts, key_of, pos_of):
        self.parts = parts
        self.key_of, self.pos_of = key_of, pos_of
        self.sub = lax.broadcasted_iota(jnp.int32, parts[0].shape, 0).astype(F32)
        self.vals, self.ids = [], []

    def step(self):
        parts = self.parts
        tree = list(parts)
        while len(tree) > 1:
            tree = [jnp.maximum(tree[i], tree[i + 1]) if i + 1 < len(tree) else tree[i]
                    for i in range(0, len(tree), 2)]
        m = jnp.max(tree[0], axis=0, keepdims=True)
        first = jnp.full(parts[0].shape, jnp.inf, F32)
        for p in reversed(range(len(parts))):
            first = jnp.where(parts[p] == m, float(p), first)
        idx = jnp.min(self.key_of(first, self.sub), axis=0, keepdims=True)
        p_hit, s_hit = self.pos_of(idx)
        hit = jnp.where(self.sub == s_hit, p_hit, -1.0)
        self.parts = [jnp.where(hit == float(p), -jnp.inf, parts[p]) for p in range(len(parts))]
        self.vals.append(m)
        self.ids.append(idx)

    def result(self):
        return jnp.concatenate(self.vals, 0), jnp.concatenate(self.ids, 0)


def _row_topk(s):
    parts = [s[p * SUBLANES:(p + 1) * SUBLANES] for p in range(s.shape[0] // SUBLANES)]

    def pos_of(idx):
        p = jnp.floor(idx * (1.0 / SUBLANES))
        return p, idx - p * SUBLANES

    return _TopK(parts, lambda first, sub: first * SUBLANES + sub, pos_of)


def _pair_topk(v1, v2):
    half = PEER_TOPK // 2
    sub = lax.broadcasted_iota(jnp.int32, (SUBLANES, v1.shape[1]), 0)
    parts = [v1[0:1] + v2[0:SUBLANES], v1[0:1] + v2[SUBLANES:]]
    for a in range(1, half):
        part = v1[a:a + 1] + v2[0:SUBLANES]
        nb_a = PEER_TOPK // (a + 1)
        parts.append(part if nb_a >= SUBLANES else jnp.where(sub < nb_a, part, -jnp.inf))
    parts.append(v1[half:] + v2[0:1])
    last = float(len(parts) - 1)

    def key_of(first, subf):
        base = jnp.where(first < 1.5, first * SUBLANES, first * PEER_TOPK - PEER_TOPK)
        return jnp.where(first == last, (subf + half) * PEER_TOPK, base + subf)

    def pos_of(idx):
        a = jnp.floor(idx * (1.0 / PEER_TOPK))
        b = idx - a * PEER_TOPK
        p = jnp.where(a >= half, last, jnp.where(a >= 1.0, a + 1.0, jnp.floor(b * (1.0 / SUBLANES))))
        s = jnp.where(a >= half, a - half, jnp.where(a >= 1.0, b, b - jnp.floor(b * (1.0 / SUBLANES)) * SUBLANES))
        return p, s

    return _TopK(parts, key_of, pos_of)


G_PITCH_PAD = 8
G_STORE_LAG = 2


def _peer_route_kernel(x_ref, wq_ref, k1_ref, k2_ref, o_ref, i1_scr, i2_scr, g_scr, buf):
    tb = LANES
    pitch = tb + G_PITCH_PAD
    nsel = PEER_HEADS * PEER_TOPK
    half = PEER_QDIM // 2

    @pl.when(pl.program_id(0) == 0)
    def _():
        i1_scr[...] = jnp.zeros_like(i1_scr)
        i2_scr[...] = jnp.zeros_like(i2_scr)
        g_scr[...] = jnp.zeros_like(g_scr)

    kidx = lax.broadcasted_iota(jnp.int32, (N_KEYS, nsel), 0).astype(F32)
    pending = {}

    def scatter_slot(slot):
        t = slot
        if t < tb:
            r1 = i1_scr[t:t + 1, :]
            r2 = i2_scr[t:t + 1, :]
            rg = g_scr[t:t + 1, :]
            at = jnp.where(kidx == r1, rg, 0.0).astype(BF16)
            bt = jnp.where(kidx == r2, 1.0, 0.0).astype(BF16)
            pending[t] = _dot_nt(at, bt)
        t = slot - G_STORE_LAG
        if t in pending:
            gt = pending.pop(t)
            for r in range(N_KEYS // SUBLANES):
                buf[pl.ds(r * SUBLANES * pitch + t, SUBLANES, stride=pitch), :] = (
                    gt[r * SUBLANES:(r + 1) * SUBLANES, :])

    q = _dot(x_ref[...].astype(BF16), wq_ref[...])
    k1 = k1_ref[...]
    k2 = k2_ref[...]

    def key_scores(h):
        lo = h * PEER_QDIM
        s1 = _dot_nt(k1, q[:, lo:lo + half].astype(BF16))
        s2 = _dot_nt(k2, q[:, lo + half:lo + PEER_QDIM].astype(BF16))
        return _row_topk(s1), _row_topk(s2)

    picks = []

    def finish(t1, t2, tc):
        _, i1 = t1.result()
        _, i2 = t2.result()
        sc, ci = tc.result()
        ca = jnp.floor(ci * (1.0 / PEER_TOPK))
        cb = ci - ca * PEER_TOPK
        e1 = jnp.zeros_like(ci)
        e2 = jnp.zeros_like(ci)
        for a in range(PEER_TOPK):
            e1 = jnp.where(ca == float(a), i1[a:a + 1], e1)
            e2 = jnp.where(cb == float(a), i2[a:a + 1], e2)
        ex = jnp.exp(sc - sc[0:1])
        gate = ex * (1.0 / jnp.sum(ex, axis=0, keepdims=True))
        picks.append((e1, e2, gate))

    slot = 0
    prev = None
    for h in range(PEER_HEADS + 1):
        cur = key_scores(h) if h < PEER_HEADS else None
        for _ in range(PEER_TOPK):
            if cur is not None:
                cur[0].step()
                cur[1].step()
            if prev is not None:
                prev[2].step()
            scatter_slot(slot)
            slot += 1
        if prev is not None:
            finish(*prev)
        if cur is not None:
            v1, _ = cur[0].result()
            v2, _ = cur[1].result()
            prev = (cur[0], cur[1], _pair_topk(v1, v2))
    while pending or slot < tb:
        scatter_slot(slot)
        slot += 1

    for i1 in range(N_KEYS):
        o_ref[i1] = buf[i1 * pitch:i1 * pitch + tb, :].astype(BF16)
    i1_scr[...] = jnp.concatenate([p[0] for p in picks], 0).T
    i2_scr[...] = jnp.concatenate([p[1] for p in picks], 0).T
    g_scr[...] = jnp.concatenate([p[2] for p in picks], 0).T


def _peer_route(x, wq_b, k1_b, k2_b):
    t = x.shape[0]
    tb = LANES
    nblk = t // tb
    nsel = PEER_HEADS * PEER_TOPK
    return pl.pallas_call(
        _peer_route_kernel,
        grid=(nblk + 1,),
        in_specs=[pl.BlockSpec((tb, D_MODEL), lambda i: (jnp.minimum(i, nblk - 1), 0)),
                  pl.BlockSpec((D_MODEL, PEER_HEADS * PEER_QDIM), lambda i: (0, 0)),
                  pl.BlockSpec((N_KEYS, PEER_QDIM // 2), lambda i: (0, 0)),
                  pl.BlockSpec((N_KEYS, PEER_QDIM // 2), lambda i: (0, 0))],
        out_specs=pl.BlockSpec((N_KEYS, tb, N_KEYS), lambda i: (0, jnp.maximum(i - 1, 0), 0)),
        out_shape=jax.ShapeDtypeStruct((N_KEYS, t, N_KEYS), BF16),
        scratch_shapes=[pltpu.VMEM((tb, nsel), F32), pltpu.VMEM((tb, nsel), F32),
                        pltpu.VMEM((tb, nsel), F32),
                        pltpu.VMEM((N_KEYS * (tb + G_PITCH_PAD), N_KEYS), F32)],
        compiler_params=_cparams(("arbitrary",)),
        name="peer_route",
    )(x, wq_b, k1_b, k2_b)


PEER_I1_PER_CHUNK = 8
PEER_CHUNK = PEER_I1_PER_CHUNK * N_KEYS


PEER_ROW_SPLIT = 2


def _gelu_tanh(x):
    c1 = math.sqrt(2.0 / math.pi)
    t = jnp.tanh(x * (c1 + (c1 * 0.044715) * (x * x)))
    hx = 0.5 * x
    return hx + hx * t


def _peer_ffn_kernel(x_ref, u_ref, v_ref, gm_ref, lg_ref, lb_ref, o_ref, xb_scr, acc_scr):
    c = pl.program_id(1)

    @pl.when(c == 0)
    def _():
        xb_scr[...] = x_ref[...].astype(BF16)
        acc_scr[...] = jnp.zeros_like(acc_scr)

    u = u_ref[...]
    v = v_ref[...]
    tm = xb_scr.shape[0]
    nrow = tm // PEER_ROW_SPLIT

    def hidden(r):
        return _dot_nt(xb_scr[r * nrow:(r + 1) * nrow, :], u)

    def weigh(r, hid):
        ws = []
        for j in range(PEER_I1_PER_CHUNK):
            hj = hid[:, j * N_KEYS:(j + 1) * N_KEYS]
            gj = gm_ref[j, r * nrow:(r + 1) * nrow, :].astype(F32)
            ws.append((_gelu_tanh(hj) * gj).astype(BF16))
        return jnp.concatenate(ws, -1)

    hids = [hidden(0)]
    for r in range(PEER_ROW_SPLIT):
        if r + 1 < PEER_ROW_SPLIT:
            hids.append(hidden(r + 1))
        w = weigh(r, hids[r])
        acc_scr[r * nrow:(r + 1) * nrow, :] += _dot(w, v)

    @pl.when(c == pl.num_programs(1) - 1)
    def _():
        o_ref[...] = _layer_norm(DEEPNORM_ALPHA * x_ref[...] + acc_scr[...], lg_ref[...], lb_ref[...])


def _peer_ffn(x, u_b, v_b, gmap, lg, lb, tm):
    t = x.shape[0]
    nchunk = u_b.shape[0] // PEER_CHUNK
    return pl.pallas_call(
        _peer_ffn_kernel,
        grid=(t // tm, nchunk),
        in_specs=[pl.BlockSpec((tm, D_MODEL), lambda i, c: (i, 0)),
                  pl.BlockSpec((PEER_CHUNK, D_MODEL), lambda i, c: (c, 0)),
                  pl.BlockSpec((PEER_CHUNK, D_MODEL), lambda i, c: (c, 0)),
                  pl.BlockSpec((PEER_I1_PER_CHUNK, tm, N_KEYS), lambda i, c: (c, i, 0)),
                  pl.BlockSpec((1, D_MODEL), lambda i, c: (0, 0)),
                  pl.BlockSpec((1, D_MODEL), lambda i, c: (0, 0))],
        out_specs=pl.BlockSpec((tm, D_MODEL), lambda i, c: (i, 0)),
        out_shape=jax.ShapeDtypeStruct((t, D_MODEL), F32),
        scratch_shapes=[pltpu.VMEM((tm, D_MODEL), BF16), pltpu.VMEM((tm, D_MODEL), F32)],
        compiler_params=_cparams(("parallel", "arbitrary")),
        name="peer_ffn",
    )(x, u_b, v_b, gmap, lg, lb)


def _peer(x2, wts, tm):
    gmap = _peer_route(x2, wts["w_pq"], wts["keys1"], wts["keys2"])
    return _peer_ffn(x2, wts["peer_u"], wts["peer_v"], gmap, wts["ln3_g"], wts["ln3_b"], tm)


def _rope_tables(pos):
    half = ATT_HEAD_DIM // 2
    inv_freq = ROPE_THETA ** (-np.arange(half, dtype=np.float64) / half)
    ang = np.asarray(pos, np.float64)[:, None] * inv_freq[None, :]
    cos = np.cos(ang).astype(np.float32)
    sin = np.sin(ang).astype(np.float32)
    reps = LANES // ATT_HEAD_DIM
    cos_t = np.tile(np.concatenate([cos, cos], -1), (1, reps))
    sin_t = np.tile(np.concatenate([-sin, sin], -1), (1, reps))
    return jnp.asarray(cos_t), jnp.asarray(sin_t)


def _prep_weights(w_in, w_conv, dn_a_log, dn_dt_bias, dn_norm_g, attn_sinks, w_o, ln1_g, ln1_b,
                  w_cq, w_ck, w_cv, w_co, ln2_g, ln2_b, w_pq, peer_keys1, peer_keys2,
                  peer_u, peer_v, ln3_g, ln3_b):
    o_dqkv = ATT_Q + 2 * ATT_KV
    o_ba = o_dqkv + CONV_CH
    o_dz = o_ba + 2 * DN_HEADS
    w_ba = w_in[:, o_ba:o_dz]
    w_p = jnp.concatenate([w_in[:, :o_ba], w_in[:, o_dz:], w_ba,
                           jnp.zeros((D_MODEL, LANES - 2 * DN_HEADS), F32)], -1).astype(BF16)
    lane_par = lambda p: jnp.zeros((LANES,), F32).at[DN_HEADS:2 * DN_HEADS].set(p)
    par_c = jnp.zeros((SUBLANES, LANES), F32).at[0].set(lane_par(dn_a_log)).at[1].set(lane_par(dn_dt_bias))
    row_par = lambda p: jnp.broadcast_to(
        jnp.zeros((SUBLANES,), F32).at[DN_HEADS:2 * DN_HEADS].set(p)[:, None], (SUBLANES, LANES))
    par_r = jnp.stack([row_par(dn_a_log), row_par(dn_dt_bias)])
    row = lambda p: p.reshape(1, -1)
    return dict(
        w_in=w_p, w_ba_t=w_ba.T.astype(BF16), w_conv=w_conv, par_c=par_c, par_r=par_r,
        norm_g=row(dn_norm_g), sinks=attn_sinks,
        w_o_a=w_o[:ATT_Q].astype(BF16), w_o_d=w_o[ATT_Q:].astype(BF16),
        ln1_g=row(ln1_g), ln1_b=row(ln1_b),
        w_cq=w_cq.astype(BF16), w_ck=w_ck.astype(BF16), w_cv=w_cv.astype(BF16), w_co=w_co.astype(BF16),
        ln2_g=row(ln2_g), ln2_b=row(ln2_b),
        w_pq=w_pq.reshape(D_MODEL, PEER_HEADS * PEER_QDIM).astype(BF16),
        keys1=peer_keys1.astype(BF16), keys2=peer_keys2.astype(BF16),
        peer_u=peer_u.astype(BF16), peer_v=peer_v.astype(BF16),
        ln3_g=row(ln3_g), ln3_b=row(ln3_b),
    )


def _tile(t, pref):
    while t % pref:
        pref //= 2
    return pref


XATTN_SEQS = 4
SWA_SEQS = 8
TOKEN_TILE = 512
PEER_TILE = 1024


def _layer_tail(xf, a_out, dn_out, mk, mv, nb, seq, wts):
    t = xf.shape[0]
    tm = _tile(t, TOKEN_TILE)
    x1, qc = _mm_ln([a_out, dn_out], [wts["w_o_a"], wts["w_o_d"]], xf, wts["ln1_g"], wts["ln1_b"], tm,
                    wts["w_cq"])
    tq = _tile(seq, 512)
    nblk = seq // tq
    xo = _xattn(qc, mk, mv, nb, nblk, tq, _tile(nb, XATTN_SEQS) if nblk == 1 else 1)
    x2 = _mm_ln([xo], [wts["w_co"]], x1, wts["ln2_g"], wts["ln2_b"], tm)
    return _peer(x2, wts, _tile(t, PEER_TILE))


def _prompt_layer(x, mem, wts):
    nb, seq, _ = x.shape
    t = nb * seq
    xf = x.reshape(t, D_MODEL)
    cos, sin = _rope_tables(np.arange(seq))
    q, k, v, dqkv, dz, ba, bat = _in_proj(xf, wts["w_in"], wts["w_ba_t"], cos, sin, _tile(seq, TOKEN_TILE))
    nblk = seq // WINDOW
    a_out = _swa(wts["sinks"], q, k, v, nb, nblk, WINDOW, WINDOW, True, 1)
    c = DN_CHUNK if seq % DN_CHUNK == 0 else seq
    nchunk = seq // c
    bat_c = bat.reshape(SUBLANES, nb * nchunk, c).transpose(1, 0, 2)
    tail = jnp.zeros((nb * SUBLANES, CONV_CH), F32)
    s0 = jnp.zeros((nb, DN_HEADS, DN_HEAD_DIM, DN_HEAD_DIM), F32)
    dn_out, s_new = _deltanet(dqkv, tail, wts["w_conv"], ba, bat_c, dz, s0, wts["par_c"], wts["par_r"],
                              wts["norm_g"], nb, nchunk, c, c)
    memf = mem.reshape(nb * MEM_LEN, D_MODEL)
    mk = _mm(memf, wts["w_ck"], _tile(nb * MEM_LEN, 256))
    mv = _mm(memf, wts["w_cv"], _tile(nb * MEM_LEN, 256))
    y = _layer_tail(xf, a_out, dn_out, mk, mv, nb, seq, wts)
    new_k = k.reshape(nb, seq, ATT_KV_HEADS, ATT_HEAD_DIM)[:, -WINDOW:]
    new_v = v.reshape(nb, seq, ATT_KV_HEADS, ATT_HEAD_DIM)[:, -WINDOW:]
    new_conv = dqkv.reshape(nb, seq, CONV_CH)[:, -(CONV_WIDTH - 1):]
    mk4 = mk.reshape(nb, MEM_LEN, X_HEADS, X_HEAD_DIM)
    mv4 = mv.reshape(nb, MEM_LEN, X_HEADS, X_HEAD_DIM)
    return y.reshape(nb, seq, D_MODEL), new_k, new_v, s_new, new_conv, mk4, mv4


def _sample_layer(x, past_k, past_v, s0, conv_buf, mem_k, mem_v, wts):
    nb, seq, _ = x.shape
    assert CONV_WIDTH - 1 <= seq <= SAMPLE_PAD
    pad = SAMPLE_PAD
    t = nb * pad
    xf = jnp.pad(x, ((0, 0), (0, pad - seq), (0, 0))).reshape(t, D_MODEL)
    cos8, sin8 = _rope_tables(PAST_LEN + np.arange(pad))
    tm = _tile(t, TOKEN_TILE)
    cos = jnp.tile(cos8, (tm // pad, 1))
    sin = jnp.tile(sin8, (tm // pad, 1))
    q, k, v, dqkv, dz, ba, bat = _in_proj(xf, wts["w_in"], wts["w_ba_t"], cos, sin, tm)
    npast = past_k.shape[1]
    kk = jnp.concatenate([past_k.reshape(nb, npast, ATT_KV), k.reshape(nb, pad, ATT_KV)], 1)
    vv = jnp.concatenate([past_v.reshape(nb, npast, ATT_KV), v.reshape(nb, pad, ATT_KV)], 1)
    nk = npast + pad
    a_out = _swa(wts["sinks"], q, kk.reshape(nb * nk, ATT_KV), vv.reshape(nb * nk, ATT_KV),
                 nb, 1, pad, nk, False, _tile(nb, SWA_SEQS))
    bat_c = bat.reshape(SUBLANES, nb, pad).transpose(1, 0, 2)
    tail = jnp.pad(conv_buf, ((0, 0), (SUBLANES - (CONV_WIDTH - 1), 0), (0, 0))).reshape(nb * SUBLANES, CONV_CH)
    dn_out, s_new = _deltanet(dqkv, tail, wts["w_conv"], ba, bat_c, dz, s0, wts["par_c"], wts["par_r"],
                              wts["norm_g"], nb, 1, pad, seq)
    mk = mem_k.reshape(nb * MEM_LEN, D_MODEL)
    mv = mem_v.reshape(nb * MEM_LEN, D_MODEL)
    y = _layer_tail(xf, a_out, dn_out, mk, mv, nb, pad, wts)
    new_k = kk[:, seq:seq + npast].reshape(nb, npast, ATT_KV_HEADS, ATT_HEAD_DIM)
    new_v = vv[:, seq:seq + npast].reshape(nb, npast, ATT_KV_HEADS, ATT_HEAD_DIM)
    new_conv = dqkv.reshape(nb, pad, CONV_CH)[:, seq - (CONV_WIDTH - 1):seq]
    return y.reshape(nb, pad, D_MODEL)[:, :seq], new_k, new_v, s_new, new_conv


def kernel(x_prompt, x_sample, cache_swa_k, cache_swa_v, state_dn, state_dn_conv, cache_mem_k, cache_mem_v, mem_prompt, w_in, w_conv, dn_a_log, dn_dt_bias, dn_norm_g, attn_sinks, w_o, ln1_g, ln1_b, w_cq, w_ck, w_cv, w_co, ln2_g, ln2_b, w_pq, peer_keys1, peer_keys2, peer_u, peer_v, ln3_g, ln3_b):
    assert w_in.shape[0] == DEPTH == 1
    wts = _prep_weights(w_in[0], w_conv[0], dn_a_log[0], dn_dt_bias[0], dn_norm_g[0], attn_sinks[0],
                        w_o[0], ln1_g[0], ln1_b[0], w_cq[0], w_ck[0], w_cv[0], w_co[0], ln2_g[0], ln2_b[0],
                        w_pq[0], peer_keys1[0], peer_keys2[0], peer_u[0], peer_v[0], ln3_g[0], ln3_b[0])
    y_p, pk, pv, ps, pc, pmk, pmv = _prompt_layer(x_prompt, mem_prompt, wts)
    y_s, sk, sv, ss, sc = _sample_layer(x_sample, cache_swa_k[0], cache_swa_v[0], state_dn[0],
                                        state_dn_conv[0], cache_mem_k[0], cache_mem_v[0], wts)
    st = lambda a: a[None]
    return (y_p, y_s, st(pk), st(pv), st(ps), st(pc), st(pmk), st(pmv), st(sk), st(sv), st(ss), st(sc))
```
